```python
import math
import jax, jax.numpy as jnp
from jax import lax
import numpy as np

D_MODEL = 1024
BATCH = 32
SEQ = 256
DEPTH = 2
DEC_BATCH = 8
DEC_SEQ = 2048
PAST_LEN = 512

GRID_W = 64
N_MIXERS = 2
N_A_LAYERS = (DEPTH + 1) // 2
N_B_LAYERS = DEPTH // 2
EPS = 1e-6
CHUNK = 128
A_WIDTH = 2 * D_MODEL
A_GROUPS = 8
A_GROUP_DIM = A_WIDTH // A_GROUPS
N_HEADS = 8
Q_LORA = 512
KV_LORA = 256
NOPE_DIM = 128
ROPE_DIM = 64
V_DIM = 128
ROPE_AXIS_PAIRS = ROPE_DIM // 4
ROPE_BASE = 10000.0
ATTN_SCALE = 1.0 / math.sqrt(NOPE_DIM + ROPE_DIM)
Q_BLOCK = 128
D_FF = ((8 * D_MODEL // 3 + 255) // 256) * 256

kernel_name = "hybrid_gmlp_mla_diffusion_step"


def _rms_norm(x, g):
    xf = x.astype(jnp.float32)
    y = xf * lax.rsqrt(jnp.mean(xf * xf, axis=-1, keepdims=True) + EPS)
    return (y * g.astype(jnp.float32)).astype(x.dtype)


def _modulate(h, shift, scale):
    return h * (1 + scale) + shift


def _axial_rope_tables(n_tokens):
    rows = n_tokens // GRID_W
    row = jnp.repeat(jnp.arange(rows), GRID_W)
    col = jnp.tile(jnp.arange(GRID_W), rows)
    inv = 1.0 / (ROPE_BASE ** (jnp.arange(ROPE_AXIS_PAIRS, dtype=jnp.float32) / ROPE_AXIS_PAIRS))
    ang = jnp.stack([row, col], axis=-1).astype(jnp.float32)[:, :, None] * inv
    ang = jnp.broadcast_to(ang[:, :, None, :], (rows * GRID_W, 2, 2, ROPE_AXIS_PAIRS))
    ang = ang.reshape(rows * GRID_W, ROPE_DIM)
    return jnp.cos(ang), jnp.sin(ang)


def _apply_rope(x, cos, sin):
    xs = x.reshape(x.shape[:-1] + (2, 2, ROPE_AXIS_PAIRS))
    rot = jnp.concatenate([-xs[..., 1:, :], xs[..., :1, :]], axis=-2).reshape(x.shape)
    return (x * cos + rot * sin).astype(x.dtype)


def _chunk_mlp(h, w_in, v_g, w_s, b_s, w_out):
    B, L, _ = h.shape
    z = jax.nn.gelu(h @ w_in, approximate=False)
    u, v = jnp.split(z, 2, axis=-1)
    v = _rms_norm(v, v_g).reshape(B, L // CHUNK, CHUNK, A_GROUPS, A_GROUP_DIM)
    s = jnp.einsum('gpq,bnqgc->bnpgc', w_s, v) + b_s.T[:, :, None]
    return (u * s.reshape(B, L, A_WIDTH)) @ w_out


def _mla_q(h, w_dq, q_g, w_uq):
    B, L, _ = h.shape
    q = (_rms_norm(h @ w_dq, q_g) @ w_uq).reshape(B, L, N_HEADS, NOPE_DIM + ROPE_DIM)
    return q[..., :NOPE_DIM], q[..., NOPE_DIM:]


def _mla_kv(h, w_dkv, kv_g):
    kv = h @ w_dkv
    return _rms_norm(kv[..., :KV_LORA], kv_g), kv[..., KV_LORA:]


def _mla_attend(q_nope, q_rope, ckv, k_rope, w_ukv, w_o):
    B, Lk, _ = ckv.shape
    Lq = q_nope.shape[1]
    kvu = (ckv @ w_ukv).reshape(B, Lk, N_HEADS, NOPE_DIM + V_DIM)
    k_nope, v = kvu[..., :NOPE_DIM], kvu[..., NOPE_DIM:]
    nb = Lq // Q_BLOCK
    qn = q_nope.reshape(B, nb, Q_BLOCK, N_HEADS, NOPE_DIM).swapaxes(0, 1)
    qr = q_rope.reshape(B, nb, Q_BLOCK, N_HEADS, ROPE_DIM).swapaxes(0, 1)

    def block(args):
        qn_b, qr_b = args
        s = (jnp.einsum('bqhd,bkhd->bhqk', qn_b, k_nope)
             + jnp.einsum('bqhr,bkr->bhqk', qr_b, k_rope))
        p = jax.nn.softmax(s.astype(jnp.float32) * ATTN_SCALE, axis=-1).astype(v.dtype)
        return jnp.einsum('bhqk,bkhd->bqhd', p, v)

    o = lax.map(block, (qn, qr))
    o = o.swapaxes(0, 1).reshape(B, Lq, N_HEADS * V_DIM)
    return o @ w_o


def _swiglu(h, w_in, w_out):
    g, u = jnp.split(h @ w_in, 2, axis=-1)
    return (jax.nn.silu(g) * u) @ w_out


def setup_inputs(seed: int = 0) -> dict:
    key = jax.random.key(seed)
    ks = iter(jax.random.split(key, 40))

    def nrm(shape, scale):
        return jax.random.normal(next(ks), shape, jnp.float32) * scale

    D = D_MODEL
    return {
        "x_prompt": nrm((BATCH, SEQ, D), 1.0),
        "x_sample": nrm((DEC_BATCH, DEC_SEQ, D), 1.0),
        "cache_ckv": nrm((DEC_BATCH, N_B_LAYERS, PAST_LEN, KV_LORA), 1.0),
        "cache_krope": nrm((DEC_BATCH, N_B_LAYERS, PAST_LEN, ROPE_DIM), 1.0),
        "c": nrm((DEC_BATCH, D), 1.0),
        "c_ctx": nrm((D,), 1.0),
        "ada_w": nrm((DEPTH, D, 6 * D), 0.5 * D ** -0.5),
        "ada_b": nrm((DEPTH, 6 * D), 0.1),
        "norm1_g": 1.0 + nrm((DEPTH, D), 0.02),
        "norm2_g": 1.0 + nrm((DEPTH, D), 0.02),
        "a_w_in": nrm((N_A_LAYERS, D, 2 * A_WIDTH), D ** -0.5),
        "a_v_g": 1.0 + nrm((N_A_LAYERS, A_WIDTH), 0.02),
        "a_w_s": nrm((N_A_LAYERS, A_GROUPS, CHUNK, CHUNK), CHUNK ** -0.5),
        "a_b_s": 1.0 + nrm((N_A_LAYERS, A_GROUPS, CHUNK), 0.02),
        "a_w_out": nrm((N_A_LAYERS, A_WIDTH, D), A_WIDTH ** -0.5),
        "b_w_dq": nrm((N_B_LAYERS, D, Q_LORA), D ** -0.5),
        "b_q_g": 1.0 + nrm((N_B_LAYERS, Q_LORA), 0.02),
        "b_w_uq": nrm((N_B_LAYERS, Q_LORA, N_HEADS * (NOPE_DIM + ROPE_DIM)), Q_LORA ** -0.5),
        "b_w_dkv": nrm((N_B_LAYERS, D, KV_LORA + ROPE_DIM), D ** -0.5),
        "b_kv_g": 1.0 + nrm((N_B_LAYERS, KV_LORA), 0.02),
        "b_w_ukv": nrm((N_B_LAYERS, KV_LORA, N_HEADS * (NOPE_DIM + V_DIM)), KV_LORA ** -0.5),
        "b_w_o": nrm((N_B_LAYERS, N_HEADS * V_DIM, D), (N_HEADS * V_DIM) ** -0.5),
        "f_w_in": nrm((DEPTH, D, 2 * D_FF), D ** -0.5),
        "f_w_out": nrm((DEPTH, D_FF, D), D_FF ** -0.5),
        "final_g": 1.0 + nrm((D,), 0.02),
    }


def reference(x_prompt, x_sample, cache_ckv, cache_krope, c, c_ctx,
              ada_w, ada_b, norm1_g, norm2_g,
              a_w_in, a_v_g, a_w_s, a_b_s, a_w_out,
              b_w_dq, b_q_g, b_w_uq, b_w_dkv, b_kv_g, b_w_ukv, b_w_o,
              f_w_in, f_w_out, final_g):
    xc = x_prompt
    ctx_ckv, ctx_kr = [], []
    for i in range(DEPTH):
        mod = (jax.nn.silu(c_ctx) @ ada_w[i] + ada_b[i])[None, None, :]
        sh1, sc1, g1, sh2, sc2, g2 = jnp.split(mod, 6, axis=-1)
        h = _modulate(_rms_norm(xc, norm1_g[i]), sh1, sc1)
        j = i // N_MIXERS
        if i % N_MIXERS == 0:
            out = _chunk_mlp(h, a_w_in[j], a_v_g[j], a_w_s[j], a_b_s[j], a_w_out[j])
        else:
            qn, qr = _mla_q(h, b_w_dq[j], b_q_g[j], b_w_uq[j])
            ckv, kr = _mla_kv(h, b_w_dkv[j], b_kv_g[j])
            ctx_ckv.append(ckv)
            ctx_kr.append(kr)
            out = _mla_attend(qn, qr, ckv, kr, b_w_ukv[j], b_w_o[j])
        xc = xc + g1 * out
        h = _modulate(_rms_norm(xc, norm2_g[i]), sh2, sc2)
        xc = xc + g2 * _swiglu(h, f_w_in[i], f_w_out[i])
    y_prompt = _rms_norm(xc, final_g)
    new_ckv = jnp.stack(ctx_ckv, axis=1)
    new_krope = jnp.stack(ctx_kr, axis=1)

    xs = x_sample
    cos, sin = _axial_rope_tables(x_sample.shape[1])
    for i in range(DEPTH):
        mod = (jax.nn.silu(c) @ ada_w[i] + ada_b[i])[:, None, :]
        sh1, sc1, g1, sh2, sc2, g2 = jnp.split(mod, 6, axis=-1)
        h = _modulate(_rms_norm(xs, norm1_g[i]), sh1, sc1)
        j = i // N_MIXERS
        if i % N_MIXERS == 0:
            out = _chunk_mlp(h, a_w_in[j], a_v_g[j], a_w_s[j], a_b_s[j], a_w_out[j])
        else:
            qn, qr = _mla_q(h, b_w_dq[j], b_q_g[j], b_w_uq[j])
            qr = _apply_rope(qr, cos[:, None, :], sin[:, None, :])
            ckv, kr = _mla_kv(h, b_w_dkv[j], b_kv_g[j])
            kr = _apply_rope(kr, cos, sin)
            ckv_all = jnp.concatenate([cache_ckv[:, j].astype(ckv.dtype), ckv], axis=1)
            kr_all = jnp.concatenate([cache_krope[:, j].astype(kr.dtype), kr], axis=1)
            out = _mla_attend(qn, qr, ckv_all, kr_all, b_w_ukv[j], b_w_o[j])
        xs = xs + g1 * out
        h = _modulate(_rms_norm(xs, norm2_g[i]), sh2, sc2)
        xs = xs + g2 * _swiglu(h, f_w_in[i], f_w_out[i])
    y_sample = _rms_norm(xs, final_g)

    return (y_prompt, y_sample, new_ckv, new_krope)
```

```python
import functools
import math

import jax
import jax.numpy as jnp
from jax import lax
from jax.experimental import pallas as pl
from jax.experimental.pallas import tpu as pltpu

F32 = jnp.float32
BF16 = jnp.bfloat16

D_MODEL = 1024
EPS = 1e-6
CHUNK = 128
A_WIDTH = 2 * D_MODEL
A_GROUPS = 8
A_GROUP_DIM = A_WIDTH // A_GROUPS
N_HEADS = 8
Q_LORA = 512
KV_LORA = 256
NOPE_DIM = 128
ROPE_DIM = 64
V_DIM = 128
ROPE_AXIS_PAIRS = ROPE_DIM // 4
ROPE_BASE = 10000.0
GRID_W = 64
ATTN_SCALE = 1.0 / math.sqrt(NOPE_DIM + ROPE_DIM)
D_FF = ((8 * D_MODEL // 3 + 255) // 256) * 256
SQRT_HALF = math.sqrt(0.5)

HEAD_W = NOPE_DIM + 2 * ROPE_DIM
MOD_ROWS = 16
MOD_TN = 1536
TOKEN_TILE = 256
SAMPLE_Q_BLOCK = 512
VMEM_LIMIT_BYTES = 56 * 1024 * 1024


def _rms(x, g):
    return x * lax.rsqrt(jnp.mean(x * x, axis=-1, keepdims=True) + EPS) * g


def _mod_chunk(mod, k):
    return mod[:, k * D_MODEL:(k + 1) * D_MODEL]


def _mod_kernel(c_ref, w_ref, b_ref, o_ref):
    s = jax.nn.silu(c_ref[...]).astype(BF16)
    o_ref[...] = jnp.dot(s, w_ref[...].astype(BF16), preferred_element_type=F32) + b_ref[...]


def _mod_call(cond, ada_w, ada_b):
    depth = ada_w.shape[0]
    n = ada_w.shape[2]
    return pl.pallas_call(
        _mod_kernel,
        grid=(depth, n // MOD_TN),
        in_specs=[
            pl.BlockSpec((MOD_ROWS, D_MODEL), lambda i, j: (0, 0)),
            pl.BlockSpec((None, D_MODEL, MOD_TN), lambda i, j: (i, 0, j)),
            pl.BlockSpec((None, 1, MOD_TN), lambda i, j: (i, 0, j)),
        ],
        out_specs=pl.BlockSpec((None, MOD_ROWS, MOD_TN), lambda i, j: (i, 0, j)),
        out_shape=jax.ShapeDtypeStruct((depth, MOD_ROWS, n), F32),
        compiler_params=pltpu.CompilerParams(
            dimension_semantics=("arbitrary", "arbitrary"), vmem_limit_bytes=VMEM_LIMIT_BYTES),
        name="mod",
    )(cond, ada_w, ada_b.reshape(depth, 1, n))


def _gmlp_kernel(x_ref, mod_ref, n1_ref, win_ref, vg_ref, ws_ref, bs_ref, wout_ref, o_ref, y_ref):
    x = x_ref[...]
    mod = mod_ref[...]
    h = _rms(x, n1_ref[...]) * (1.0 + _mod_chunk(mod, 1)) + _mod_chunk(mod, 0)
    z = jnp.dot(h.astype(BF16), win_ref[...], preferred_element_type=F32)
    z = 0.5 * z * (1.0 + lax.erf(z * SQRT_HALF))
    u = z[:, :A_WIDTH]
    v = _rms(z[:, A_WIDTH:], vg_ref[...]).astype(BF16)
    for c in range(x.shape[0] // CHUNK):
        rows = slice(c * CHUNK, (c + 1) * CHUNK)
        for g in range(A_GROUPS):
            cols = slice(g * A_GROUP_DIM, (g + 1) * A_GROUP_DIM)
            s = jnp.dot(ws_ref[g], v[rows, cols], preferred_element_type=F32) + bs_ref[g]
            y_ref[rows, cols] = (u[rows, cols] * s).astype(BF16)
    out = jnp.dot(y_ref[...], wout_ref[...], preferred_element_type=F32)
    o_ref[...] = x + _mod_chunk(mod, 2) * out


def _ffn_kernel(*refs, has_attn, final):
    refs = list(refs)
    x_ref, mod_ref = refs[:2]
    pos = 2
    if has_attn:
        a_ref, wo_ref = refs[pos:pos + 2]
        pos += 2
    n2_ref, win_ref, wout_ref = refs[pos:pos + 3]
    pos += 3
    if final:
        fg_ref = refs[pos]
        pos += 1
    o_ref = refs[pos]

    x = x_ref[...]
    mod = mod_ref[...]
    if has_attn:
        x = x + _mod_chunk(mod, 2) * jnp.dot(a_ref[...], wo_ref[...], preferred_element_type=F32)
    h = _rms(x, n2_ref[...]) * (1.0 + _mod_chunk(mod, 4)) + _mod_chunk(mod, 3)
    gu = jnp.dot(h.astype(BF16), win_ref[...], preferred_element_type=F32)
    a = (jax.nn.silu(gu[:, :D_FF]) * gu[:, D_FF:]).astype(BF16)
    y = x + _mod_chunk(mod, 5) * jnp.dot(a, wout_ref[...], preferred_element_type=F32)
    if final:
        y = _rms(y, fg_ref[...])
    o_ref[...] = y


def _store_kv(ckv_bf16, krz_bf16, wukv_ref, k_ref, v_ref):
    kvu = jnp.dot(ckv_bf16, wukv_ref[...], preferred_element_type=F32)
    for hh in range(N_HEADS):
        k_ref[:, hh * HEAD_W:hh * HEAD_W + NOPE_DIM] = kvu[:, hh * NOPE_DIM:(hh + 1) * NOPE_DIM].astype(BF16)
        k_ref[:, hh * HEAD_W + NOPE_DIM:(hh + 1) * HEAD_W] = krz_bf16
    v_ref[...] = kvu[:, N_HEADS * NOPE_DIM:].astype(BF16)


def _proj_kernel(*refs, rope):
    refs = list(refs)
    x_ref, mod_ref, n1_ref, wdq_ref, qg_ref, wuq_ref, wdkv_ref, kvg_ref, wukv_ref = refs[:9]
    pos = 9
    if rope:
        cos_ref, sin_ref = refs[pos:pos + 2]
        pos += 2
    q_ref, k_ref, v_ref = refs[pos:pos + 3]
    pos += 3
    if not rope:
        ckv_ref, kr_ref = refs[pos:pos + 2]

    x = x_ref[...]
    mod = mod_ref[...]
    h = (_rms(x, n1_ref[...]) * (1.0 + _mod_chunk(mod, 1)) + _mod_chunk(mod, 0)).astype(BF16)
    ql = _rms(jnp.dot(h, wdq_ref[...], preferred_element_type=F32), qg_ref[...]).astype(BF16)
    q = jnp.dot(ql, wuq_ref[...], preferred_element_type=F32)
    kv = jnp.dot(h, wdkv_ref[...], preferred_element_type=F32)
    ckv = _rms(kv[:, :KV_LORA], kvg_ref[...])
    krb = kv[:, KV_LORA:]

    if rope:
        cs = cos_ref[...]
        sn = sin_ref[...]

        def rot(b):
            return b * cs + pltpu.roll(b, ROPE_DIM, 1) * sn
    else:
        keep = (lax.broadcasted_iota(jnp.int32, (1, 2 * ROPE_DIM), 1) < ROPE_DIM).astype(F32)

        def rot(b):
            return b * keep

    for hh in range(N_HEADS):
        lo = hh * HEAD_W
        q_ref[:, lo:lo + NOPE_DIM] = (q[:, lo:lo + NOPE_DIM] * ATTN_SCALE).astype(BF16)
        q_ref[:, lo + NOPE_DIM:lo + HEAD_W] = (rot(q[:, lo + NOPE_DIM:lo + HEAD_W]) * ATTN_SCALE).astype(BF16)
    _store_kv(ckv.astype(BF16), rot(krb).astype(BF16), wukv_ref, k_ref, v_ref)
    if not rope:
        ckv_ref[...] = ckv
        kr_ref[...] = krb[:, :ROPE_DIM]


def _cachekv_kernel(ckv_ref, krz_ref, wukv_ref, k_ref, v_ref):
    _store_kv(ckv_ref[...].astype(BF16), krz_ref[...].astype(BF16), wukv_ref, k_ref, v_ref)


def _attn_kernel(*refs, heads, has_cache):
    refs = list(refs)
    q_ref, kn_ref, vn_ref = refs[:3]
    if has_cache:
        kc_ref, vc_ref = refs[3:5]
    o_ref = refs[-1]
    nt = (((1,), (1,)), ((), ()))
    for hh in range(heads):
        kcols = slice(hh * HEAD_W, (hh + 1) * HEAD_W)
        vcols = slice(hh * V_DIM, (hh + 1) * V_DIM)
        q = q_ref[:, kcols]
        s = lax.dot_general(q, kn_ref[:, kcols], nt, preferred_element_type=F32)
        m = jnp.max(s, axis=-1, keepdims=True)
        if has_cache:
            sc = lax.dot_general(q, kc_ref[:, kcols], nt, preferred_element_type=F32)
            m = jnp.maximum(m, jnp.max(sc, axis=-1, keepdims=True))
        p = jnp.exp(s - m)
        l = jnp.sum(p, axis=-1, keepdims=True)
        acc = jnp.dot(p.astype(BF16), vn_ref[:, vcols], preferred_element_type=F32)
        if has_cache:
            pc = jnp.exp(sc - m)
            l = l + jnp.sum(pc, axis=-1, keepdims=True)
            acc = acc + jnp.dot(pc.astype(BF16), vc_ref[:, vcols], preferred_element_type=F32)
        o_ref[:, vcols] = (acc / l).astype(BF16)


def _const_spec(arr):
    zeros = (0,) * arr.ndim
    return pl.BlockSpec(arr.shape, lambda *_: zeros, pipeline_mode=pl.Buffered(1))


def _token_params():
    return pltpu.CompilerParams(dimension_semantics=("arbitrary",), vmem_limit_bytes=VMEM_LIMIT_BYTES)


def _rotate_half_cols(w):
    ws = w.reshape(w.shape[:-1] + (2, 2, ROPE_AXIS_PAIRS))
    return jnp.concatenate([-ws[..., 1:, :], ws[..., :1, :]], axis=-2).reshape(w.shape)


def _axial_rope_tables(n_tokens):
    rows = n_tokens // GRID_W
    row = jnp.repeat(jnp.arange(rows), GRID_W)
    col = jnp.tile(jnp.arange(GRID_W), rows)
    inv = 1.0 / (ROPE_BASE ** (jnp.arange(ROPE_AXIS_PAIRS, dtype=F32) / ROPE_AXIS_PAIRS))
    ang = jnp.stack([row, col], axis=-1).astype(F32)[:, :, None] * inv
    ang = jnp.broadcast_to(ang[:, :, None, :], (rows * GRID_W, 2, 2, ROPE_AXIS_PAIRS))
    ang = ang.reshape(rows * GRID_W, ROPE_DIM)
    pad = ((0, 0), (0, ROPE_DIM))
    return jnp.pad(jnp.cos(ang), pad), jnp.pad(jnp.sin(ang), pad)


class _Stream:
    def __init__(self, x, first_mod_row, shared_mod):
        self.batch, self.seq, _ = x.shape
        self.x = x.reshape(self.batch * self.seq, D_MODEL)
        self.tokens = self.batch * self.seq
        self.tiles = self.tokens // TOKEN_TILE
        tiles_per_batch = self.seq // TOKEN_TILE
        if shared_mod:
            self.mod_row = lambda t: first_mod_row
        else:
            self.mod_row = lambda t: first_mod_row + t // tiles_per_batch
        self.tiles_per_batch = tiles_per_batch

    def tile_spec(self, width):
        return pl.BlockSpec((TOKEN_TILE, width), lambda t: (t, 0))

    def mod_spec(self, layer):
        return pl.BlockSpec((None, None, 1, 6 * D_MODEL), lambda t: (layer, self.mod_row(t), 0, 0))


def _gmlp_call(st, x, mod4, layer, n1, w_in, v_g, w_s, b_s, w_out):
    consts = (n1, w_in, v_g, w_s, b_s, w_out)
    return pl.pallas_call(
        _gmlp_kernel,
        grid=(st.tiles,),
        in_specs=[st.tile_spec(D_MODEL), st.mod_spec(layer)] + [_const_spec(a) for a in consts],
        out_specs=st.tile_spec(D_MODEL),
        out_shape=jax.ShapeDtypeStruct((st.tokens, D_MODEL), F32),
        scratch_shapes=[pltpu.VMEM((TOKEN_TILE, A_WIDTH), BF16)],
        compiler_params=_token_params(),
        name="gmlp",
    )(x, mod4, *consts)


def _ffn_call(st, x, mod4, layer, n2, w_in, w_out, attn=None, w_o=None, final_g=None):
    has_attn = attn is not None
    final = final_g is not None
    args = [x, mod4]
    specs = [st.tile_spec(D_MODEL), st.mod_spec(layer)]
    if has_attn:
        args += [attn, w_o]
        specs += [st.tile_spec(N_HEADS * V_DIM), _const_spec(w_o)]
    consts = [n2, w_in, w_out] + ([final_g] if final else [])
    args += consts
    specs += [_const_spec(a) for a in consts]
    return pl.pallas_call(
        functools.partial(_ffn_kernel, has_attn=has_attn, final=final),
        grid=(st.tiles,),
        in_specs=specs,
        out_specs=st.tile_spec(D_MODEL),
        out_shape=jax.ShapeDtypeStruct((st.tokens, D_MODEL), F32),
        compiler_params=_token_params(),
        name="ffn",
    )(*args)


def _proj_call(st, x, mod4, layer, n1, w_dq, q_g, w_uq, w_dkv, kv_g, w_ukv, rope_tables=None):
    rope = rope_tables is not None
    consts = (n1, w_dq, q_g, w_uq, w_dkv, kv_g, w_ukv)
    args = [x, mod4, *consts]
    specs = [st.tile_spec(D_MODEL), st.mod_spec(layer)] + [_const_spec(a) for a in consts]
    out_shape = [
        jax.ShapeDtypeStruct((st.tokens, N_HEADS * HEAD_W), BF16),
        jax.ShapeDtypeStruct((st.tokens, N_HEADS * HEAD_W), BF16),
        jax.ShapeDtypeStruct((st.tokens, N_HEADS * V_DIM), BF16),
    ]
    out_specs = [st.tile_spec(N_HEADS * HEAD_W), st.tile_spec(N_HEADS * HEAD_W), st.tile_spec(N_HEADS * V_DIM)]
    if rope:
        tpb = st.tiles_per_batch
        table_spec = pl.BlockSpec((TOKEN_TILE, 2 * ROPE_DIM), lambda t: (t % tpb, 0))
        args += list(rope_tables)
        specs += [table_spec, table_spec]
    else:
        out_shape += [jax.ShapeDtypeStruct((st.tokens, KV_LORA), F32),
                      jax.ShapeDtypeStruct((st.tokens, ROPE_DIM), F32)]
        out_specs += [st.tile_spec(KV_LORA), st.tile_spec(ROPE_DIM)]
    return pl.pallas_call(
        functools.partial(_proj_kernel, rope=rope),
        grid=(st.tiles,),
        in_specs=specs,
        out_specs=out_specs,
        out_shape=out_shape,
        compiler_params=_token_params(),
        name="proj",
    )(*args)


def _cachekv_call(ckv, krz, w_ukv):
    tokens = ckv.shape[0]
    tile = lambda width: pl.BlockSpec((TOKEN_TILE, width), lambda t: (t, 0))
    return pl.pallas_call(
        _cachekv_kernel,
        grid=(tokens // TOKEN_TILE,),
        in_specs=[tile(KV_LORA), tile(2 * ROPE_DIM), _const_spec(w_ukv)],
        out_specs=[tile(N_HEADS * HEAD_W), tile(N_HEADS * V_DIM)],
        out_shape=[jax.ShapeDtypeStruct((tokens, N_HEADS * HEAD_W), BF16),
                   jax.ShapeDtypeStruct((tokens, N_HEADS * V_DIM), BF16)],
        compiler_params=_token_params(),
        name="cachekv",
    )(ckv, krz, w_ukv)


def _attn_call(batch, seq, q, k_new, v_new, k_cache=None, v_cache=None, *, q_block, heads):
    has_cache = k_cache is not None
    nq = seq // q_block
    groups = N_HEADS // heads
    kw, vw = heads * HEAD_W, heads * V_DIM
    args = [q, k_new, v_new]
    specs = [
        pl.BlockSpec((q_block, kw), lambda b, g, i: (b * nq + i, g)),
        pl.BlockSpec((seq, kw), lambda b, g, i: (b, g)),
        pl.BlockSpec((seq, vw), lambda b, g, i: (b, g)),
    ]
    if has_cache:
        past = k_cache.shape[0] // batch
        args += [k_cache, v_cache]
        specs += [pl.BlockSpec((past, kw), lambda b, g, i: (b, g)),
                  pl.BlockSpec((past, vw), lambda b, g, i: (b, g))]
    return pl.pallas_call(
        functools.partial(_attn_kernel, heads=heads, has_cache=has_cache),
        grid=(batch, groups, nq),
        in_specs=specs,
        out_specs=pl.BlockSpec((q_block, vw), lambda b, g, i: (b * nq + i, g)),
        out_shape=jax.ShapeDtypeStruct((batch * seq, N_HEADS * V_DIM), BF16),
        compiler_params=pltpu.CompilerParams(
            dimension_semantics=("arbitrary", "arbitrary", "arbitrary"), vmem_limit_bytes=VMEM_LIMIT_BYTES),
        name="attn",
    )(*args)


def kernel(x_prompt, x_sample, cache_ckv, cache_krope, c, c_ctx, ada_w, ada_b, norm1_g, norm2_g, a_w_in, a_v_g, a_w_s, a_b_s, a_w_out, b_w_dq, b_q_g, b_w_uq, b_w_dkv, b_kv_g, b_w_ukv, b_w_o, f_w_in, f_w_out, final_g):
    dec_batch = x_sample.shape[0]
    row = lambda g: g.reshape(1, -1)

    cond = jnp.concatenate(
        [c_ctx[None, :], c, jnp.zeros((MOD_ROWS - 1 - dec_batch, D_MODEL), F32)], axis=0)
    mod = _mod_call(cond, ada_w, ada_b)
    mod4 = mod.reshape(mod.shape[0], MOD_ROWS, 1, 6 * D_MODEL)

    a_w_in_b = a_w_in[0].astype(BF16)
    a_w_s_b = a_w_s[0].astype(BF16)
    a_b_s_c = a_b_s[0][:, :, None]
    a_w_out_b = a_w_out[0].astype(BF16)
    f_w_in_b = f_w_in.astype(BF16)
    f_w_out_b = f_w_out.astype(BF16)
    w_dq = b_w_dq[0].astype(BF16)
    uq = b_w_uq[0].reshape(Q_LORA, N_HEADS, NOPE_DIM + ROPE_DIM)
    uq_rope = uq[..., NOPE_DIM:]
    w_uq = jnp.concatenate([uq[..., :NOPE_DIM], uq_rope, _rotate_half_cols(uq_rope)], axis=-1)
    w_uq = w_uq.reshape(Q_LORA, N_HEADS * HEAD_W).astype(BF16)
    dkv_rope = b_w_dkv[0][:, KV_LORA:]
    w_dkv = jnp.concatenate([b_w_dkv[0], _rotate_half_cols(dkv_rope)], axis=-1).astype(BF16)
    ukv = b_w_ukv[0].reshape(KV_LORA, N_HEADS, NOPE_DIM + V_DIM)
    w_ukv = jnp.concatenate([ukv[..., :NOPE_DIM].reshape(KV_LORA, -1),
                             ukv[..., NOPE_DIM:].reshape(KV_LORA, -1)], axis=-1).astype(BF16)
    w_o = b_w_o[0].astype(BF16)

    outs = []
    for x_in, first_row, shared in ((x_prompt, 0, True), (x_sample, 1, False)):
        st = _Stream(x_in, first_row, shared)
        is_sample = not shared
        x = _gmlp_call(st, st.x, mod4, 0, row(norm1_g[0]), a_w_in_b, row(a_v_g[0]), a_w_s_b, a_b_s_c, a_w_out_b)
        x = _ffn_call(st, x, mod4, 0, row(norm2_g[0]), f_w_in_b[0], f_w_out_b[0])
        proj_consts = (row(norm1_g[1]), w_dq, row(b_q_g[0]), w_uq, w_dkv, row(b_kv_g[0]), w_ukv)
        if is_sample:
            q, k_new, v_new = _proj_call(st, x, mod4, 1, *proj_consts,
                                         rope_tables=_axial_rope_tables(st.seq))
            past = cache_ckv.shape[2]
            krz = jnp.pad(cache_krope[:, 0].reshape(st.batch * past, ROPE_DIM), ((0, 0), (0, ROPE_DIM)))
            k_cache, v_cache = _cachekv_call(cache_ckv[:, 0].reshape(st.batch * past, KV_LORA), krz, w_ukv)
            attn = _attn_call(st.batch, st.seq, q, k_new, v_new, k_cache, v_cache,
                              q_block=SAMPLE_Q_BLOCK, heads=1)
        else:
            q, k_new, v_new, ckv, kr = _proj_call(st, x, mod4, 1, *proj_consts)
            attn = _attn_call(st.batch, st.seq, q, k_new, v_new, q_block=st.seq, heads=N_HEADS)
            outs += [ckv.reshape(st.batch, 1, st.seq, KV_LORA), kr.reshape(st.batch, 1, st.seq, ROPE_DIM)]
        y = _ffn_call(st, x, mod4, 1, row(norm2_g[1]), f_w_in_b[1], f_w_out_b[1],
                      attn=attn, w_o=w_o, final_g=row(final_g))
        outs.append(y.reshape(x_in.shape))

    new_ckv, new_krope, y_prompt, y_sample = outs
    return (y_prompt, y_sample, new_ckv, new_krope)
```

```python
import functools
import math

import jax
import jax.numpy as jnp
from jax import lax
from jax.experimental import pallas as pl
from jax.experimental.pallas import tpu as pltpu

F32 = jnp.float32
BF16 = jnp.bfloat16

D_MODEL = 1024
EPS = 1e-6
CHUNK = 128
A_WIDTH = 2 * D_MODEL
A_GROUPS = 8
A_GROUP_DIM = A_WIDTH // A_GROUPS
N_HEADS = 8
Q_LORA = 512
KV_LORA = 256
NOPE_DIM = 128
ROPE_DIM = 64
V_DIM = 128
ROPE_AXIS_PAIRS = ROPE_DIM // 4
ROPE_BASE = 10000.0
GRID_W = 64
ATTN_SCALE = 1.0 / math.sqrt(NOPE_DIM + ROPE_DIM)
Q_SCALE = ATTN_SCALE * math.log2(math.e)
D_FF = ((8 * D_MODEL // 3 + 255) // 256) * 256
SQRT_HALF = math.sqrt(0.5)

HEAD_W = NOPE_DIM + 2 * ROPE_DIM
MOD_ROWS = 16
MOD_TN = 1536
TOKEN_TILE = 512
SAMPLE_Q_BLOCK = 2048
ATTN_SUB_ROWS = 256
ATTN_KEY_CHUNK = 512
VMEM_LIMIT_BYTES = 56 * 1024 * 1024


def _rms(x, g):
    return x * lax.rsqrt(jnp.mean(x * x, axis=-1, keepdims=True) + EPS) * g


def _mod_chunk(mod, k):
    return mod[:, k * D_MODEL:(k + 1) * D_MODEL]


def _mod_kernel(c_ref, w_ref, b_ref, o_ref):
    s = jax.nn.silu(c_ref[...]).astype(BF16)
    o_ref[...] = jnp.dot(s, w_ref[...].astype(BF16), preferred_element_type=F32) + b_ref[...]


def _mod_call(cond, ada_w, ada_b):
    depth = ada_w.shape[0]
    n = ada_w.shape[2]
    return pl.pallas_call(
        _mod_kernel,
        grid=(depth, n // MOD_TN),
        in_specs=[
            pl.BlockSpec((MOD_ROWS, D_MODEL), lambda i, j: (0, 0)),
            pl.BlockSpec((None, D_MODEL, MOD_TN), lambda i, j: (i, 0, j)),
            pl.BlockSpec((None, 1, MOD_TN), lambda i, j: (i, 0, j)),
        ],
        out_specs=pl.BlockSpec((None, MOD_ROWS, MOD_TN), lambda i, j: (i, 0, j)),
        out_shape=jax.ShapeDtypeStruct((depth, MOD_ROWS, n), F32),
        compiler_params=pltpu.CompilerParams(
            dimension_semantics=("arbitrary", "arbitrary"), vmem_limit_bytes=VMEM_LIMIT_BYTES),
        name="mod",
    )(cond, ada_w, ada_b.reshape(depth, 1, n))


def _gmlp_kernel(x_ref, mod_ref, n1_ref, win_ref, vg_ref, ws_ref, bs_ref, wout_ref, o_ref, y_ref):
    x = x_ref[...]
    mod = mod_ref[...]
    h = _rms(x, n1_ref[...]) * (1.0 + _mod_chunk(mod, 1)) + _mod_chunk(mod, 0)
    z = jnp.dot(h.astype(BF16), win_ref[...], preferred_element_type=F32)
    z = 0.5 * z * (1.0 + lax.erf(z * SQRT_HALF))
    u = z[:, :A_WIDTH]
    v = _rms(z[:, A_WIDTH:], vg_ref[...]).astype(BF16)
    for c in range(x.shape[0] // CHUNK):
        rows = slice(c * CHUNK, (c + 1) * CHUNK)
        for g in range(A_GROUPS):
            cols = slice(g * A_GROUP_DIM, (g + 1) * A_GROUP_DIM)
            s = jnp.dot(ws_ref[g], v[rows, cols], preferred_element_type=F32) + bs_ref[g]
            y_ref[rows, cols] = (u[rows, cols] * s).astype(BF16)
    out = jnp.dot(y_ref[...], wout_ref[...], preferred_element_type=F32)
    o_ref[...] = x + _mod_chunk(mod, 2) * out


def _ffn_kernel(*refs, has_attn, final):
    refs = list(refs)
    x_ref, mod_ref = refs[:2]
    pos = 2
    if has_attn:
        a_ref, wo_ref = refs[pos:pos + 2]
        pos += 2
    n2_ref, win_ref, wout_ref = refs[pos:pos + 3]
    pos += 3
    if final:
        fg_ref = refs[pos]
        pos += 1
    o_ref = refs[pos]

    x = x_ref[...]
    mod = mod_ref[...]
    if has_attn:
        x = x + _mod_chunk(mod, 2) * jnp.dot(a_ref[...], wo_ref[...], preferred_element_type=F32)
    h = _rms(x, n2_ref[...]) * (1.0 + _mod_chunk(mod, 4)) + _mod_chunk(mod, 3)
    gu = jnp.dot(h.astype(BF16), win_ref[...], preferred_element_type=F32)
    a = (jax.nn.silu(gu[:, :D_FF]) * gu[:, D_FF:]).astype(BF16)
    y = x + _mod_chunk(mod, 5) * jnp.dot(a, wout_ref[...], preferred_element_type=F32)
    if final:
        y = _rms(y, fg_ref[...])
    o_ref[...] = y


_NT = (((1,), (1,)), ((), ()))


def _store_kv(ckv_bf16, krz_bf16, wuk_ref, wuv_ref, k_ref, v_ref, v_transposed):
    kn = jnp.dot(ckv_bf16, wuk_ref[...], preferred_element_type=F32)
    for hh in range(N_HEADS):
        k_ref[:, hh * HEAD_W:hh * HEAD_W + NOPE_DIM] = kn[:, hh * NOPE_DIM:(hh + 1) * NOPE_DIM].astype(BF16)
        k_ref[:, hh * HEAD_W + NOPE_DIM:(hh + 1) * HEAD_W] = krz_bf16
    if v_transposed:
        v = lax.dot_general(wuv_ref[...], ckv_bf16, _NT, preferred_element_type=F32)
    else:
        v = jnp.dot(ckv_bf16, wuv_ref[...], preferred_element_type=F32)
    v_ref[...] = v.astype(BF16)


def _proj_kernel(*refs, rope):
    refs = list(refs)
    x_ref, mod_ref, n1_ref, wdq_ref, qg_ref, wuq_ref, wdkv_ref, kvg_ref, wuk_ref, wuv_ref = refs[:10]
    pos = 10
    if rope:
        cos_ref, sin_ref = refs[pos:pos + 2]
        pos += 2
    q_ref, k_ref, v_ref = refs[pos:pos + 3]
    pos += 3
    if not rope:
        ckv_ref, kr_ref = refs[pos:pos + 2]

    x = x_ref[...]
    mod = mod_ref[...]
    h = (_rms(x, n1_ref[...]) * (1.0 + _mod_chunk(mod, 1)) + _mod_chunk(mod, 0)).astype(BF16)
    ql = _rms(jnp.dot(h, wdq_ref[...], preferred_element_type=F32), qg_ref[...]).astype(BF16)
    q = jnp.dot(ql, wuq_ref[...], preferred_element_type=F32)
    kv = jnp.dot(h, wdkv_ref[...], preferred_element_type=F32)
    ckv = _rms(kv[:, :KV_LORA], kvg_ref[...])
    krb = kv[:, KV_LORA:]

    if rope:
        cs = cos_ref[...]
        sn = sin_ref[...]

        def rot(b):
            return b * cs + pltpu.roll(b, ROPE_DIM, 1) * sn
    else:
        keep = (lax.broadcasted_iota(jnp.int32, (1, 2 * ROPE_DIM), 1) < ROPE_DIM).astype(F32)

        def rot(b):
            return b * keep

    for hh in range(N_HEADS):
        lo = hh * HEAD_W
        q_ref[:, lo:lo + NOPE_DIM] = (q[:, lo:lo + NOPE_DIM] * Q_SCALE).astype(BF16)
        q_ref[:, lo + NOPE_DIM:lo + HEAD_W] = (rot(q[:, lo + NOPE_DIM:lo + HEAD_W]) * Q_SCALE).astype(BF16)
    _store_kv(ckv.astype(BF16), rot(krb).astype(BF16), wuk_ref, wuv_ref, k_ref, v_ref, v_transposed=rope)
    if not rope:
        ckv_ref[...] = ckv
        kr_ref[...] = krb[:, :ROPE_DIM]


def _cachekv_kernel(ckv_ref, krz_ref, wuk_ref, wuvt_ref, k_ref, vt_ref):
    _store_kv(ckv_ref[...].astype(BF16), krz_ref[...].astype(BF16), wuk_ref, wuvt_ref, k_ref, vt_ref,
              v_transposed=True)


def _attn_kernel(q_ref, k_ref, v_ref, o_ref):
    for hh in range(N_HEADS):
        kcols = slice(hh * HEAD_W, (hh + 1) * HEAD_W)
        vcols = slice(hh * V_DIM, (hh + 1) * V_DIM)
        s = lax.dot_general(q_ref[:, kcols], k_ref[:, kcols], _NT, preferred_element_type=F32)
        p = jnp.exp2(s - jnp.max(s, axis=-1, keepdims=True))
        l = jnp.sum(p, axis=-1, keepdims=True)
        acc = jnp.dot(p.astype(BF16), v_ref[:, vcols], preferred_element_type=F32)
        o_ref[:, vcols] = (acc / l).astype(BF16)


def _fold8(x, op):
    return op(x.reshape(x.shape[0] // 8, 8, x.shape[1]), axis=0)


def _attn_pipe_kernel(q_ref, kn_ref, vtn_ref, kc_ref, vtc_ref, o_ref, st0_ref, st1_ref, m0_ref, m1_ref):
    sources = ((kn_ref, vtn_ref), (kc_ref, vtc_ref))
    n_sub = q_ref.shape[0] // ATTN_SUB_ROWS

    def chunks():
        off = 0
        for k_ref, vt_ref in sources:
            for c in range(k_ref.shape[0] // ATTN_KEY_CHUNK):
                keys = slice(c * ATTN_KEY_CHUNK, (c + 1) * ATTN_KEY_CHUNK)
                yield k_ref, vt_ref, keys, slice(off, off + ATTN_KEY_CHUNK)
                off += ATTN_KEY_CHUNK

    def sub_rows(u):
        if isinstance(u, int):
            return slice(u * ATTN_SUB_ROWS, (u + 1) * ATTN_SUB_ROWS)
        return pl.ds(pl.multiple_of(u * ATTN_SUB_ROWS, ATTN_SUB_ROWS), ATTN_SUB_ROWS)

    bufs = ((st0_ref, m0_ref), (st1_ref, m1_ref))

    def score_phase(u, slot):
        st_ref, m_ref = bufs[slot]
        q = q_ref[sub_rows(u), :]
        m8 = None
        for k_ref, _, keys, srows in chunks():
            st = lax.dot_general(k_ref[keys, :], q, _NT, preferred_element_type=F32)
            st_ref[srows, :] = st
            part = _fold8(st, jnp.max)
            m8 = part if m8 is None else jnp.maximum(m8, part)
        m_ref[...] = jnp.max(m8, axis=0, keepdims=True)

    def value_phase(u, slot):
        st_ref, m_ref = bufs[slot]
        m = m_ref[...]
        l8 = None
        acc = None
        for _, vt_ref, keys, srows in chunks():
            pt = jnp.exp2(st_ref[srows, :] - m)
            part = _fold8(pt, jnp.sum)
            pv = jnp.dot(vt_ref[:, keys], pt.astype(BF16), preferred_element_type=F32)
            l8 = part if l8 is None else l8 + part
            acc = pv if acc is None else acc + pv
        l = jnp.sum(l8, axis=0, keepdims=True)
        o_ref[sub_rows(u), :] = (acc / l).T.astype(BF16)

    score_phase(0, 0)

    def body(j, carry):
        u = 2 * j + 1
        score_phase(u, 1)
        value_phase(u - 1, 0)
        score_phase(u + 1, 0)
        value_phase(u, 1)
        return carry

    lax.fori_loop(0, (n_sub - 2) // 2, body, 0)
    score_phase(n_sub - 1, 1)
    value_phase(n_sub - 2, 0)
    value_phase(n_sub - 1, 1)


def _const_spec(arr):
    zeros = (0,) * arr.ndim
    return pl.BlockSpec(arr.shape, lambda *_: zeros, pipeline_mode=pl.Buffered(1))


def _token_params():
    return pltpu.CompilerParams(dimension_semantics=("arbitrary",), vmem_limit_bytes=VMEM_LIMIT_BYTES)


def _rotate_half_cols(w):
    ws = w.reshape(w.shape[:-1] + (2, 2, ROPE_AXIS_PAIRS))
    return jnp.concatenate([-ws[..., 1:, :], ws[..., :1, :]], axis=-2).reshape(w.shape)


def _axial_rope_tables(n_tokens):
    rows = n_tokens // GRID_W
    row = jnp.repeat(jnp.arange(rows), GRID_W)
    col = jnp.tile(jnp.arange(GRID_W), rows)
    inv = 1.0 / (ROPE_BASE ** (jnp.arange(ROPE_AXIS_PAIRS, dtype=F32) / ROPE_AXIS_PAIRS))
    ang = jnp.stack([row, col], axis=-1).astype(F32)[:, :, None] * inv
    ang = jnp.broadcast_to(ang[:, :, None, :], (rows * GRID_W, 2, 2, ROPE_AXIS_PAIRS))
    ang = ang.reshape(rows * GRID_W, ROPE_DIM)
    pad = ((0, 0), (0, ROPE_DIM))
    return jnp.pad(jnp.cos(ang), pad), jnp.pad(jnp.sin(ang), pad)


class _Stream:
    def __init__(self, x, first_mod_row, shared_mod):
        self.batch, self.seq, _ = x.shape
        self.x = x.reshape(self.batch * self.seq, D_MODEL)
        self.tokens = self.batch * self.seq
        self.tiles = self.tokens // TOKEN_TILE
        tiles_per_batch = self.seq // TOKEN_TILE
        if shared_mod:
            self.mod_row = lambda t: first_mod_row
        else:
            self.mod_row = lambda t: first_mod_row + t // tiles_per_batch
        self.tiles_per_batch = tiles_per_batch

    def tile_spec(self, width):
        return pl.BlockSpec((TOKEN_TILE, width), lambda t: (t, 0))

    def mod_spec(self, layer):
        return pl.BlockSpec((None, None, 1, 6 * D_MODEL), lambda t: (layer, self.mod_row(t), 0, 0))


def _gmlp_call(st, x, mod4, layer, n1, w_in, v_g, w_s, b_s, w_out):
    consts = (n1, w_in, v_g, w_s, b_s, w_out)
    return pl.pallas_call(
        _gmlp_kernel,
        grid=(st.tiles,),
        in_specs=[st.tile_spec(D_MODEL), st.mod_spec(layer)] + [_const_spec(a) for a in consts],
        out_specs=st.tile_spec(D_MODEL),
        out_shape=jax.ShapeDtypeStruct((st.tokens, D_MODEL), F32),
        scratch_shapes=[pltpu.VMEM((TOKEN_TILE, A_WIDTH), BF16)],
        compiler_params=_token_params(),
        name="gmlp",
    )(x, mod4, *consts)


def _ffn_call(st, x, mod4, layer, n2, w_in, w_out, attn=None, w_o=None, final_g=None):
    has_attn = attn is not None
    final = final_g is not None
    args = [x, mod4]
    specs = [st.tile_spec(D_MODEL), st.mod_spec(layer)]
    if has_attn:
        args += [attn, w_o]
        specs += [st.tile_spec(N_HEADS * V_DIM), _const_spec(w_o)]
    consts = [n2, w_in, w_out] + ([final_g] if final else [])
    args += consts
    specs += [_const_spec(a) for a in consts]
    return pl.pallas_call(
        functools.partial(_ffn_kernel, has_attn=has_attn, final=final),
        grid=(st.tiles,),
        in_specs=specs,
        out_specs=st.tile_spec(D_MODEL),
        out_shape=jax.ShapeDtypeStruct((st.tokens, D_MODEL), F32),
        compiler_params=_token_params(),
        name="ffn",
    )(*args)


def _vt_tile_spec():
    return pl.BlockSpec((N_HEADS * V_DIM, TOKEN_TILE), lambda t: (0, t))


def _proj_call(st, x, mod4, layer, n1, w_dq, q_g, w_uq, w_dkv, kv_g, w_uk, w_uv, rope_tables=None):
    rope = rope_tables is not None
    consts = (n1, w_dq, q_g, w_uq, w_dkv, kv_g, w_uk, w_uv)
    args = [x, mod4, *consts]
    specs = [st.tile_spec(D_MODEL), st.mod_spec(layer)] + [_const_spec(a) for a in consts]
    v_shape = (N_HEADS * V_DIM, st.tokens) if rope else (st.tokens, N_HEADS * V_DIM)
    out_shape = [
        jax.ShapeDtypeStruct((st.tokens, N_HEADS * HEAD_W), BF16),
        jax.ShapeDtypeStruct((st.tokens, N_HEADS * HEAD_W), BF16),
        jax.ShapeDtypeStruct(v_shape, BF16),
    ]
    out_specs = [st.tile_spec(N_HEADS * HEAD_W), st.tile_spec(N_HEADS * HEAD_W),
                 _vt_tile_spec() if rope else st.tile_spec(N_HEADS * V_DIM)]
    if rope:
        tpb = st.tiles_per_batch
        table_spec = pl.BlockSpec((TOKEN_TILE, 2 * ROPE_DIM), lambda t: (t % tpb, 0))
        args += list(rope_tables)
        specs += [table_spec, table_spec]
    else:
        out_shape += [jax.ShapeDtypeStruct((st.tokens, KV_LORA), F32),
                      jax.ShapeDtypeStruct((st.tokens, ROPE_DIM), F32)]
        out_specs += [st.tile_spec(KV_LORA), st.tile_spec(ROPE_DIM)]
    return pl.pallas_call(
        functools.partial(_proj_kernel, rope=rope),
        grid=(st.tiles,),
        in_specs=specs,
        out_specs=out_specs,
        out_shape=out_shape,
        compiler_params=_token_params(),
        name="proj",
    )(*args)


def _cachekv_call(ckv, krz, w_uk, w_uvt):
    tokens = ckv.shape[0]
    tile = lambda width: pl.BlockSpec((TOKEN_TILE, width), lambda t: (t, 0))
    return pl.pallas_call(
        _cachekv_kernel,
        grid=(tokens // TOKEN_TILE,),
        in_specs=[tile(KV_LORA), tile(2 * ROPE_DIM), _const_spec(w_uk), _const_spec(w_uvt)],
        out_specs=[tile(N_HEADS * HEAD_W), _vt_tile_spec()],
        out_shape=[jax.ShapeDtypeStruct((tokens, N_HEADS * HEAD_W), BF16),
                   jax.ShapeDtypeStruct((N_HEADS * V_DIM, tokens), BF16)],
        compiler_params=_token_params(),
        name="cachekv",
    )(ckv, krz, w_uk, w_uvt)


def _attn_call(batch, seq, q, k, v):
    spec = lambda width: pl.BlockSpec((seq, width), lambda b: (b, 0))
    return pl.pallas_call(
        _attn_kernel,
        grid=(batch,),
        in_specs=[spec(N_HEADS * HEAD_W), spec(N_HEADS * HEAD_W), spec(N_HEADS * V_DIM)],
        out_specs=spec(N_HEADS * V_DIM),
        out_shape=jax.ShapeDtypeStruct((batch * seq, N_HEADS * V_DIM), BF16),
        compiler_params=_token_params(),
        name="attn",
    )(q, k, v)


def _attn_pipe_call(batch, seq, q, k_new, vt_new, k_cache, vt_cache):
    past = k_cache.shape[0] // batch
    return pl.pallas_call(
        _attn_pipe_kernel,
        grid=(batch, N_HEADS),
        in_specs=[
            pl.BlockSpec((seq, HEAD_W), lambda b, h: (b, h)),
            pl.BlockSpec((seq, HEAD_W), lambda b, h: (b, h)),
            pl.BlockSpec((V_DIM, seq), lambda b, h: (h, b)),
            pl.BlockSpec((past, HEAD_W), lambda b, h: (b, h)),
            pl.BlockSpec((V_DIM, past), lambda b, h: (h, b)),
        ],
        out_specs=pl.BlockSpec((seq, V_DIM), lambda b, h: (b, h)),
        out_shape=jax.ShapeDtypeStruct((batch * seq, N_HEADS * V_DIM), BF16),
        scratch_shapes=[pltpu.VMEM((seq + past, ATTN_SUB_ROWS), F32)] * 2
        + [pltpu.VMEM((1, ATTN_SUB_ROWS), F32)] * 2,
        compiler_params=pltpu.CompilerParams(
            dimension_semantics=("arbitrary", "arbitrary"), vmem_limit_bytes=VMEM_LIMIT_BYTES),
        name="attn_pipe",
    )(q, k_new, vt_new, k_cache, vt_cache)


def kernel(x_prompt, x_sample, cache_ckv, cache_krope, c, c_ctx, ada_w, ada_b, norm1_g, norm2_g, a_w_in, a_v_g, a_w_s, a_b_s, a_w_out, b_w_dq, b_q_g, b_w_uq, b_w_dkv, b_kv_g, b_w_ukv, b_w_o, f_w_in, f_w_out, final_g):
    dec_batch = x_sample.shape[0]
    row = lambda g: g.reshape(1, -1)

    cond = jnp.concatenate(
        [c_ctx[None, :], c, jnp.zeros((MOD_ROWS - 1 - dec_batch, D_MODEL), F32)], axis=0)
    mod = _mod_call(cond, ada_w, ada_b)
    mod4 = mod.reshape(mod.shape[0], MOD_ROWS, 1, 6 * D_MODEL)

    a_w_in_b = a_w_in[0].astype(BF16)
    a_w_s_b = a_w_s[0].astype(BF16)
    a_b_s_c = a_b_s[0][:, :, None]
    a_w_out_b = a_w_out[0].astype(BF16)
    f_w_in_b = f_w_in.astype(BF16)
    f_w_out_b = f_w_out.astype(BF16)
    w_dq = b_w_dq[0].astype(BF16)
    uq = b_w_uq[0].reshape(Q_LORA, N_HEADS, NOPE_DIM + ROPE_DIM)
    uq_rope = uq[..., NOPE_DIM:]
    w_uq = jnp.concatenate([uq[..., :NOPE_DIM], uq_rope, _rotate_half_cols(uq_rope)], axis=-1)
    w_uq = w_uq.reshape(Q_LORA, N_HEADS * HEAD_W).astype(BF16)
    dkv_rope = b_w_dkv[0][:, KV_LORA:]
    w_dkv = jnp.concatenate([b_w_dkv[0], _rotate_half_cols(dkv_rope)], axis=-1).astype(BF16)
    ukv = b_w_ukv[0].reshape(KV_LORA, N_HEADS, NOPE_DIM + V_DIM)
    w_uk = ukv[..., :NOPE_DIM].reshape(KV_LORA, N_HEADS * NOPE_DIM).astype(BF16)
    w_uv = ukv[..., NOPE_DIM:].reshape(KV_LORA, N_HEADS * V_DIM).astype(BF16)
    w_uvt = w_uv.T
    w_o = b_w_o[0].astype(BF16)

    outs = []
    for x_in, first_row, shared in ((x_prompt, 0, True), (x_sample, 1, False)):
        st = _Stream(x_in, first_row, shared)
        is_sample = not shared
        x = _gmlp_call(st, st.x, mod4, 0, row(norm1_g[0]), a_w_in_b, row(a_v_g[0]), a_w_s_b, a_b_s_c, a_w_out_b)
        x = _ffn_call(st, x, mod4, 0, row(norm2_g[0]), f_w_in_b[0], f_w_out_b[0])
        proj_consts = (row(norm1_g[1]), w_dq, row(b_q_g[0]), w_uq, w_dkv, row(b_kv_g[0]), w_uk)
        if is_sample:
            q, k_new, vt_new = _proj_call(st, x, mod4, 1, *proj_consts, w_uvt,
                                          rope_tables=_axial_rope_tables(st.seq))
            past = cache_ckv.shape[2]
            krz = jnp.pad(cache_krope[:, 0].reshape(st.batch * past, ROPE_DIM), ((0, 0), (0, ROPE_DIM)))
            k_cache, vt_cache = _cachekv_call(
                cache_ckv[:, 0].reshape(st.batch * past, KV_LORA), krz, w_uk, w_uvt)
            attn = _attn_pipe_call(st.batch, st.seq, q, k_new, vt_new, k_cache, vt_cache)
        else:
            q, k_new, v_new, ckv, kr = _proj_call(st, x, mod4, 1, *proj_consts, w_uv)
            attn = _attn_call(st.batch, st.seq, q, k_new, v_new)
            outs += [ckv.reshape(st.batch, 1, st.seq, KV_LORA), kr.reshape(st.batch, 1, st.seq, ROPE_DIM)]
        y = _ffn_call(st, x, mod4, 1, row(norm2_g[1]), f_w_in_b[1], f_w_out_b[1],
                      attn=attn, w_o=w_o, final_g=row(final_g))
        outs.append(y.reshape(x_in.shape))

    new_ckv, new_krope, y_prompt, y_sample = outs
    return (y_prompt, y_sample, new_ckv, new_krope)
```

```python
import functools
import math

import jax
import jax.numpy as jnp
from jax import lax
from jax.experimental import pallas as pl
from jax.experimental.pallas import tpu as pltpu

F32 = jnp.float32
BF16 = jnp.bfloat16

D_MODEL = 1024
EPS = 1e-6
CHUNK = 128
A_WIDTH = 2 * D_MODEL
A_GROUPS = 8
A_GROUP_DIM = A_WIDTH // A_GROUPS
N_HEADS = 8
Q_LORA = 512
KV_LORA = 256
NOPE_DIM = 128
ROPE_DIM = 64
V_DIM = 128
ROPE_AXIS_PAIRS = ROPE_DIM // 4
ROPE_BASE = 10000.0
GRID_W = 64
ATTN_SCALE = 1.0 / math.sqrt(NOPE_DIM + ROPE_DIM)
Q_SCALE = ATTN_SCALE * math.log2(math.e)
D_FF = ((8 * D_MODEL // 3 + 255) // 256) * 256
SQRT_HALF = math.sqrt(0.5)

HEAD_W = NOPE_DIM + 2 * ROPE_DIM
MOD_ROWS = 16
MOD_TN = 1536
TOKEN_TILE = 512
ATTN_SUB_ROWS = 256
ATTN_KEY_CHUNK = 512
ATTN_PIPE_HEADS = 4
ATTN_BATCH_PER_STEP = 8
VMEM_LIMIT_BYTES = 56 * 1024 * 1024


def _rms(x, g):
    return x * lax.rsqrt(jnp.mean(x * x, axis=-1, keepdims=True) + EPS) * g


def _gelu(z):
    return 0.5 * z * (1.0 + lax.erf(z * SQRT_HALF))


def _mod_chunk(mod, k):
    return mod[:, k * D_MODEL:(k + 1) * D_MODEL]


def _mod_kernel(c_ref, w_ref, b_ref, o_ref):
    s = jax.nn.silu(c_ref[...]).astype(BF16)
    o_ref[...] = jnp.dot(s, w_ref[...].astype(BF16), preferred_element_type=F32) + b_ref[...]


def _mod_call(cond, ada_w, ada_b):
    depth = ada_w.shape[0]
    n = ada_w.shape[2]
    return pl.pallas_call(
        _mod_kernel,
        grid=(depth, n // MOD_TN),
        in_specs=[
            pl.BlockSpec((MOD_ROWS, D_MODEL), lambda i, j: (0, 0)),
            pl.BlockSpec((None, D_MODEL, MOD_TN), lambda i, j: (i, 0, j)),
            pl.BlockSpec((None, 1, MOD_TN), lambda i, j: (i, 0, j)),
        ],
        out_specs=pl.BlockSpec((None, MOD_ROWS, MOD_TN), lambda i, j: (i, 0, j)),
        out_shape=jax.ShapeDtypeStruct((depth, MOD_ROWS, n), F32),
        compiler_params=pltpu.CompilerParams(
            dimension_semantics=("arbitrary", "arbitrary"), vmem_limit_bytes=VMEM_LIMIT_BYTES),
        name="mod",
    )(cond, ada_w, ada_b.reshape(depth, 1, n))


def _gmlp_kernel(x_ref, mod_ref, n1_ref, win_ref, vg_ref, ws_ref, bs_ref, wout_ref, o_ref, y_ref):
    x = x_ref[...]
    mod = mod_ref[...]
    h = _rms(x, n1_ref[...]) * (1.0 + _mod_chunk(mod, 1)) + _mod_chunk(mod, 0)
    hb = h.astype(BF16)
    v = jnp.dot(hb, win_ref[:, A_WIDTH:], preferred_element_type=F32)
    v = _rms(_gelu(v), vg_ref[...]).astype(BF16)
    u = _gelu(jnp.dot(hb, win_ref[:, :A_WIDTH], preferred_element_type=F32))
    for c in range(x.shape[0] // CHUNK):
        rows = slice(c * CHUNK, (c + 1) * CHUNK)
        for g in range(A_GROUPS):
            cols = slice(g * A_GROUP_DIM, (g + 1) * A_GROUP_DIM)
            s = jnp.dot(ws_ref[g], v[rows, cols], preferred_element_type=F32) + bs_ref[g]
            y_ref[rows, cols] = (u[rows, cols] * s).astype(BF16)
    out = jnp.dot(y_ref[...], wout_ref[...], preferred_element_type=F32)
    o_ref[...] = x + _mod_chunk(mod, 2) * out


def _ffn_kernel(*refs, has_attn, final):
    refs = list(refs)
    x_ref, mod_ref = refs[:2]
    pos = 2
    if has_attn:
        a_ref, wo_ref = refs[pos:pos + 2]
        pos += 2
    n2_ref, win_ref, wout_ref = refs[pos:pos + 3]
    pos += 3
    if final:
        fg_ref = refs[pos]
        pos += 1
    o_ref = refs[pos]

    x = x_ref[...]
    mod = mod_ref[...]
    if has_attn:
        x = x + _mod_chunk(mod, 2) * jnp.dot(a_ref[...], wo_ref[...], preferred_element_type=F32)
    h = _rms(x, n2_ref[...]) * (1.0 + _mod_chunk(mod, 4)) + _mod_chunk(mod, 3)
    gu = jnp.dot(h.astype(BF16), win_ref[...], preferred_element_type=F32)
    a = (jax.nn.silu(gu[:, :D_FF]) * gu[:, D_FF:]).astype(BF16)
    y = x + _mod_chunk(mod, 5) * jnp.dot(a, wout_ref[...], preferred_element_type=F32)
    if final:
        y = _rms(y, fg_ref[...])
    o_ref[...] = y


_NT = (((1,), (1,)), ((), ()))


def _store_kv(ckv_bf16, krz_bf16, wuk_ref, wuv_ref, k_ref, v_ref, pipe_layout):
    kn = jnp.dot(ckv_bf16, wuk_ref[...], preferred_element_type=F32)
    for hh in range(N_HEADS):
        kn_h = kn[:, hh * NOPE_DIM:(hh + 1) * NOPE_DIM].astype(BF16)
        if pipe_layout:
            k_ref[hh, :, :NOPE_DIM] = kn_h
            k_ref[hh, :, NOPE_DIM:] = krz_bf16
        else:
            k_ref[:, hh * HEAD_W:hh * HEAD_W + NOPE_DIM] = kn_h
            k_ref[:, hh * HEAD_W + NOPE_DIM:(hh + 1) * HEAD_W] = krz_bf16
    if pipe_layout:
        v = lax.dot_general(wuv_ref[...], ckv_bf16, _NT, preferred_element_type=F32)
    else:
        v = jnp.dot(ckv_bf16, wuv_ref[...], preferred_element_type=F32)
    v_ref[...] = v.astype(BF16)


def _proj_kernel(*refs, rope):
    refs = list(refs)
    x_ref, mod_ref, n1_ref, wdq_ref, qg_ref, wuq_ref, wdkv_ref, kvg_ref, wuk_ref, wuv_ref = refs[:10]
    pos = 10
    if rope:
        cos_ref, sin_ref, cost_ref, sint_ref = refs[pos:pos + 4]
        pos += 4
    q_ref, k_ref, v_ref = refs[pos:pos + 3]
    pos += 3
    if not rope:
        ckv_ref, kr_ref = refs[pos:pos + 2]

    x = x_ref[...]
    mod = mod_ref[...]
    h = (_rms(x, n1_ref[...]) * (1.0 + _mod_chunk(mod, 1)) + _mod_chunk(mod, 0)).astype(BF16)
    ql = _rms(jnp.dot(h, wdq_ref[...], preferred_element_type=F32), qg_ref[...]).astype(BF16)
    kv = jnp.dot(h, wdkv_ref[...], preferred_element_type=F32)
    ckv = _rms(kv[:, :KV_LORA], kvg_ref[...])
    krb = kv[:, KV_LORA:]

    if rope:
        cs, sn = cos_ref[...], sin_ref[...]
        cst, snt = cost_ref[...], sint_ref[...]
        krz = krb * cs + pltpu.roll(krb, ROPE_DIM, 1) * sn
        qt = lax.dot_general(wuq_ref[...], ql, _NT, preferred_element_type=F32)
        for hh in range(N_HEADS):
            lo = hh * HEAD_W
            q_ref[lo:lo + NOPE_DIM, :] = (qt[lo:lo + NOPE_DIM, :] * Q_SCALE).astype(BF16)
            b = qt[lo + NOPE_DIM:lo + HEAD_W, :]
            roped = b * cst + pltpu.roll(b, ROPE_DIM, 0) * snt
            q_ref[lo + NOPE_DIM:lo + HEAD_W, :] = (roped * Q_SCALE).astype(BF16)
    else:
        keep = (lax.broadcasted_iota(jnp.int32, (1, 2 * ROPE_DIM), 1) < ROPE_DIM).astype(F32)
        krz = krb * keep
        q = jnp.dot(ql, wuq_ref[...], preferred_element_type=F32)
        for hh in range(N_HEADS):
            lo = hh * HEAD_W
            q_ref[:, lo:lo + NOPE_DIM] = (q[:, lo:lo + NOPE_DIM] * Q_SCALE).astype(BF16)
            q_ref[:, lo + NOPE_DIM:lo + HEAD_W] = (q[:, lo + NOPE_DIM:lo + HEAD_W] * (keep * Q_SCALE)).astype(BF16)
    _store_kv(ckv.astype(BF16), krz.astype(BF16), wuk_ref, wuv_ref, k_ref, v_ref, pipe_layout=rope)
    if not rope:
        ckv_ref[...] = ckv
        kr_ref[...] = krb[:, :ROPE_DIM]


def _cachekv_kernel(ckv_ref, krz_ref, wuk_ref, wuvt_ref, k_ref, vt_ref):
    _store_kv(ckv_ref[...].astype(BF16), krz_ref[...].astype(BF16), wuk_ref, wuvt_ref, k_ref, vt_ref,
              pipe_layout=True)


def _attn_kernel(q_ref, k_ref, v_ref, o_ref, *, seq):
    for b in range(q_ref.shape[0] // seq):
        rows = slice(b * seq, (b + 1) * seq)
        for hh in range(N_HEADS):
            kcols = slice(hh * HEAD_W, (hh + 1) * HEAD_W)
            vcols = slice(hh * V_DIM, (hh + 1) * V_DIM)
            s = lax.dot_general(q_ref[rows, kcols], k_ref[rows, kcols], _NT, preferred_element_type=F32)
            p = jnp.exp2(s - jnp.max(s, axis=-1, keepdims=True))
            l = jnp.sum(p, axis=-1, keepdims=True)
            acc = jnp.dot(p.astype(BF16), v_ref[rows, vcols], preferred_element_type=F32)
            o_ref[rows, vcols] = (acc / l).astype(BF16)


def _fold8(x, op):
    return op(x.reshape(x.shape[0] // 8, 8, x.shape[1]), axis=0)


def _attn_pipe_kernel(qt_ref, kn_ref, vtn_ref, kc_ref, vtc_ref, o_ref, st0_ref, st1_ref, m0_ref, m1_ref):
    sources = ((kn_ref, vtn_ref), (kc_ref, vtc_ref))
    heads = qt_ref.shape[0] // HEAD_W
    units = [(hh, r) for hh in range(heads) for r in range(qt_ref.shape[1] // ATTN_SUB_ROWS)]
    bufs = ((st0_ref, m0_ref), (st1_ref, m1_ref))

    def chunks():
        off = 0
        for k_ref, vt_ref in sources:
            for c in range(k_ref.shape[1] // ATTN_KEY_CHUNK):
                keys = slice(c * ATTN_KEY_CHUNK, (c + 1) * ATTN_KEY_CHUNK)
                yield k_ref, vt_ref, keys, slice(off, off + ATTN_KEY_CHUNK)
                off += ATTN_KEY_CHUNK

    def sub_rows(r):
        return slice(r * ATTN_SUB_ROWS, (r + 1) * ATTN_SUB_ROWS)

    def score_phase(i):
        hh, r = units[i]
        st_ref, m_ref = bufs[i % 2]
        qt = qt_ref[hh * HEAD_W:(hh + 1) * HEAD_W, sub_rows(r)]
        m8 = None
        for k_ref, _, keys, srows in chunks():
            st = jnp.dot(k_ref[hh, keys, :], qt, preferred_element_type=F32)
            st_ref[srows, :] = st
            part = _fold8(st, jnp.max)
            m8 = part if m8 is None else jnp.maximum(m8, part)
        m_ref[...] = jnp.max(m8, axis=0, keepdims=True)

    def value_phase(i):
        hh, r = units[i]
        st_ref, m_ref = bufs[i % 2]
        vrows = slice(hh * V_DIM, (hh + 1) * V_DIM)
        m = m_ref[...]
        l8 = None
        acc = None
        for _, vt_ref, keys, srows in chunks():
            pt = jnp.exp2(st_ref[srows, :] - m)
            part = _fold8(pt, jnp.sum)
            pv = jnp.dot(vt_ref[vrows, keys], pt.astype(BF16), preferred_element_type=F32)
            l8 = part if l8 is None else l8 + part
            acc = pv if acc is None else acc + pv
        l = jnp.sum(l8, axis=0, keepdims=True)
        o_ref[sub_rows(r), vrows] = (acc / l).T.astype(BF16)

    score_phase(0)
    for i in range(1, len(units)):
        score_phase(i)
        value_phase(i - 1)
    value_phase(len(units) - 1)


def _const_spec(arr):
    zeros = (0,) * arr.ndim
    return pl.BlockSpec(arr.shape, lambda *_: zeros, pipeline_mode=pl.Buffered(1))


def _token_params():
    return pltpu.CompilerParams(dimension_semantics=("arbitrary",), vmem_limit_bytes=VMEM_LIMIT_BYTES)


def _rotate_half_cols(w):
    ws = w.reshape(w.shape[:-1] + (2, 2, ROPE_AXIS_PAIRS))
    return jnp.concatenate([-ws[..., 1:, :], ws[..., :1, :]], axis=-2).reshape(w.shape)


def _axial_rope_tables(n_tokens):
    rows = n_tokens // GRID_W
    row = jnp.repeat(jnp.arange(rows), GRID_W)
    col = jnp.tile(jnp.arange(GRID_W), rows)
    inv = 1.0 / (ROPE_BASE ** (jnp.arange(ROPE_AXIS_PAIRS, dtype=F32) / ROPE_AXIS_PAIRS))
    ang = jnp.stack([row, col], axis=-1).astype(F32)[:, :, None] * inv
    ang = jnp.broadcast_to(ang[:, :, None, :], (rows * GRID_W, 2, 2, ROPE_AXIS_PAIRS))
    ang = ang.reshape(rows * GRID_W, ROPE_DIM)
    pad = ((0, 0), (0, ROPE_DIM))
    return jnp.pad(jnp.cos(ang), pad), jnp.pad(jnp.sin(ang), pad)


class _Stream:
    def __init__(self, x, first_mod_row, shared_mod):
        self.batch, self.seq, _ = x.shape
        self.x = x.reshape(self.batch * self.seq, D_MODEL)
        self.tokens = self.batch * self.seq
        self.tiles = self.tokens // TOKEN_TILE
        tiles_per_batch = self.seq // TOKEN_TILE
        if shared_mod:
            self.mod_row = lambda t: first_mod_row
        else:
            self.mod_row = lambda t: first_mod_row + t // tiles_per_batch
        self.tiles_per_batch = tiles_per_batch

    def tile_spec(self, width):
        return pl.BlockSpec((TOKEN_TILE, width), lambda t: (t, 0))

    def mod_spec(self, layer):
        return pl.BlockSpec((None, None, 1, 6 * D_MODEL), lambda t: (layer, self.mod_row(t), 0, 0))


def _gmlp_call(st, x, mod4, layer, n1, w_in, v_g, w_s, b_s, w_out):
    consts = (n1, w_in, v_g, w_s, b_s, w_out)
    return pl.pallas_call(
        _gmlp_kernel,
        grid=(st.tiles,),
        in_specs=[st.tile_spec(D_MODEL), st.mod_spec(layer)] + [_const_spec(a) for a in consts],
        out_specs=st.tile_spec(D_MODEL),
        out_shape=jax.ShapeDtypeStruct((st.tokens, D_MODEL), F32),
        scratch_shapes=[pltpu.VMEM((TOKEN_TILE, A_WIDTH), BF16)],
        compiler_params=_token_params(),
        name="gmlp",
    )(x, mod4, *consts)


def _ffn_call(st, x, mod4, layer, n2, w_in, w_out, attn=None, w_o=None, final_g=None):
    has_attn = attn is not None
    final = final_g is not None
    args = [x, mod4]
    specs = [st.tile_spec(D_MODEL), st.mod_spec(layer)]
    if has_attn:
        args += [attn, w_o]
        specs += [st.tile_spec(N_HEADS * V_DIM), _const_spec(w_o)]
    consts = [n2, w_in, w_out] + ([final_g] if final else [])
    args += consts
    specs += [_const_spec(a) for a in consts]
    return pl.pallas_call(
        functools.partial(_ffn_kernel, has_attn=has_attn, final=final),
        grid=(st.tiles,),
        in_specs=specs,
        out_specs=st.tile_spec(D_MODEL),
        out_shape=jax.ShapeDtypeStruct((st.tokens, D_MODEL), F32),
        compiler_params=_token_params(),
        name="ffn",
    )(*args)


def _transposed_tile_spec(features):
    return pl.BlockSpec((features, TOKEN_TILE), lambda t: (0, t))


def _head_major_tile_spec():
    return pl.BlockSpec((N_HEADS, TOKEN_TILE, HEAD_W), lambda t: (0, t, 0))


def _proj_call(st, x, mod4, layer, n1, w_dq, q_g, w_uq, w_dkv, kv_g, w_uk, w_uv, rope_tables=None):
    rope = rope_tables is not None
    consts = (n1, w_dq, q_g, w_uq, w_dkv, kv_g, w_uk, w_uv)
    args = [x, mod4, *consts]
    specs = [st.tile_spec(D_MODEL), st.mod_spec(layer)] + [_const_spec(a) for a in consts]
    qw, vw = N_HEADS * HEAD_W, N_HEADS * V_DIM
    out_shape = [
        jax.ShapeDtypeStruct((qw, st.tokens) if rope else (st.tokens, qw), BF16),
        jax.ShapeDtypeStruct((N_HEADS, st.tokens, HEAD_W) if rope else (st.tokens, qw), BF16),
        jax.ShapeDtypeStruct((vw, st.tokens) if rope else (st.tokens, vw), BF16),
    ]
    out_specs = [_transposed_tile_spec(qw) if rope else st.tile_spec(qw),
                 _head_major_tile_spec() if rope else st.tile_spec(qw),
                 _transposed_tile_spec(vw) if rope else st.tile_spec(vw)]
    if rope:
        tpb = st.tiles_per_batch
        cos, sin = rope_tables
        table_spec = pl.BlockSpec((TOKEN_TILE, 2 * ROPE_DIM), lambda t: (t % tpb, 0))
        table_t_spec = pl.BlockSpec((2 * ROPE_DIM, TOKEN_TILE), lambda t: (0, t % tpb))
        args += [cos, sin, cos.T, sin.T]
        specs += [table_spec, table_spec, table_t_spec, table_t_spec]
    else:
        out_shape += [jax.ShapeDtypeStruct((st.tokens, KV_LORA), F32),
                      jax.ShapeDtypeStruct((st.tokens, ROPE_DIM), F32)]
        out_specs += [st.tile_spec(KV_LORA), st.tile_spec(ROPE_DIM)]
    return pl.pallas_call(
        functools.partial(_proj_kernel, rope=rope),
        grid=(st.tiles,),
        in_specs=specs,
        out_specs=out_specs,
        out_shape=out_shape,
        compiler_params=_token_params(),
        name="proj",
    )(*args)


def _cachekv_call(ckv, krz, w_uk, w_uvt):
    tokens = ckv.shape[0]
    tile = lambda width: pl.BlockSpec((TOKEN_TILE, width), lambda t: (t, 0))
    return pl.pallas_call(
        _cachekv_kernel,
        grid=(tokens // TOKEN_TILE,),
        in_specs=[tile(KV_LORA), tile(2 * ROPE_DIM), _const_spec(w_uk), _const_spec(w_uvt)],
        out_specs=[_head_major_tile_spec(), _transposed_tile_spec(N_HEADS * V_DIM)],
        out_shape=[jax.ShapeDtypeStruct((N_HEADS, tokens, HEAD_W), BF16),
                   jax.ShapeDtypeStruct((N_HEADS * V_DIM, tokens), BF16)],
        compiler_params=_token_params(),
        name="cachekv",
    )(ckv, krz, w_uk, w_uvt)


def _attn_call(batch, seq, q, k, v):
    rows = ATTN_BATCH_PER_STEP * seq
    spec = lambda width: pl.BlockSpec((rows, width), lambda b: (b, 0))
    return pl.pallas_call(
        functools.partial(_attn_kernel, seq=seq),
        grid=(batch // ATTN_BATCH_PER_STEP,),
        in_specs=[spec(N_HEADS * HEAD_W), spec(N_HEADS * HEAD_W), spec(N_HEADS * V_DIM)],
        out_specs=spec(N_HEADS * V_DIM),
        out_shape=jax.ShapeDtypeStruct((batch * seq, N_HEADS * V_DIM), BF16),
        compiler_params=_token_params(),
        name="attn",
    )(q, k, v)


def _attn_pipe_call(batch, seq, q, k_new, vt_new, k_cache, vt_cache):
    past = k_cache.shape[1] // batch
    kw, vw = ATTN_PIPE_HEADS * HEAD_W, ATTN_PIPE_HEADS * V_DIM
    return pl.pallas_call(
        _attn_pipe_kernel,
        grid=(batch, N_HEADS // ATTN_PIPE_HEADS),
        in_specs=[
            pl.BlockSpec((kw, seq), lambda b, h: (h, b)),
            pl.BlockSpec((ATTN_PIPE_HEADS, seq, HEAD_W), lambda b, h: (h, b, 0)),
            pl.BlockSpec((vw, seq), lambda b, h: (h, b)),
            pl.BlockSpec((ATTN_PIPE_HEADS, past, HEAD_W), lambda b, h: (h, b, 0)),
            pl.BlockSpec((vw, past), lambda b, h: (h, b)),
        ],
        out_specs=pl.BlockSpec((seq, vw), lambda b, h: (b, h)),
        out_shape=jax.ShapeDtypeStruct((batch * seq, N_HEADS * V_DIM), BF16),
        scratch_shapes=[pltpu.VMEM((seq + past, ATTN_SUB_ROWS), F32)] * 2
        + [pltpu.VMEM((1, ATTN_SUB_ROWS), F32)] * 2,
        compiler_params=pltpu.CompilerParams(
            dimension_semantics=("arbitrary", "arbitrary"), vmem_limit_bytes=VMEM_LIMIT_BYTES),
        name="attn_pipe",
    )(q, k_new, vt_new, k_cache, vt_cache)


def kernel(x_prompt, x_sample, cache_ckv, cache_krope, c, c_ctx, ada_w, ada_b, norm1_g, norm2_g, a_w_in, a_v_g, a_w_s, a_b_s, a_w_out, b_w_dq, b_q_g, b_w_uq, b_w_dkv, b_kv_g, b_w_ukv, b_w_o, f_w_in, f_w_out, final_g):
    dec_batch = x_sample.shape[0]
    row = lambda g: g.reshape(1, -1)

    cond = jnp.concatenate(
        [c_ctx[None, :], c, jnp.zeros((MOD_ROWS - 1 - dec_batch, D_MODEL), F32)], axis=0)
    mod = _mod_call(cond, ada_w, ada_b)
    mod4 = mod.reshape(mod.shape[0], MOD_ROWS, 1, 6 * D_MODEL)

    a_w_in_b = a_w_in[0].astype(BF16)
    a_w_s_b = a_w_s[0].astype(BF16)
    a_b_s_c = a_b_s[0][:, :, None]
    a_w_out_b = a_w_out[0].astype(BF16)
    f_w_in_b = f_w_in.astype(BF16)
    f_w_out_b = f_w_out.astype(BF16)
    w_dq = b_w_dq[0].astype(BF16)
    uq = b_w_uq[0].reshape(Q_LORA, N_HEADS, NOPE_DIM + ROPE_DIM)
    uq_rope = uq[..., NOPE_DIM:]
    w_uq = jnp.concatenate([uq[..., :NOPE_DIM], uq_rope, _rotate_half_cols(uq_rope)], axis=-1)
    w_uq = w_uq.reshape(Q_LORA, N_HEADS * HEAD_W).astype(BF16)
    dkv_rope = b_w_dkv[0][:, KV_LORA:]
    w_dkv = jnp.concatenate([b_w_dkv[0], _rotate_half_cols(dkv_rope)], axis=-1).astype(BF16)
    ukv = b_w_ukv[0].reshape(KV_LORA, N_HEADS, NOPE_DIM + V_DIM)
    w_uk = ukv[..., :NOPE_DIM].reshape(KV_LORA, N_HEADS * NOPE_DIM).astype(BF16)
    w_uv = ukv[..., NOPE_DIM:].reshape(KV_LORA, N_HEADS * V_DIM).astype(BF16)
    w_uvt = w_uv.T
    w_o = b_w_o[0].astype(BF16)

    outs = []
    for x_in, first_row, shared in ((x_prompt, 0, True), (x_sample, 1, False)):
        st = _Stream(x_in, first_row, shared)
        is_sample = not shared
        x = _gmlp_call(st, st.x, mod4, 0, row(norm1_g[0]), a_w_in_b, row(a_v_g[0]), a_w_s_b, a_b_s_c, a_w_out_b)
        x = _ffn_call(st, x, mod4, 0, row(norm2_g[0]), f_w_in_b[0], f_w_out_b[0])
        proj_head = (row(norm1_g[1]), w_dq, row(b_q_g[0]))
        proj_tail = (w_dkv, row(b_kv_g[0]), w_uk)
        if is_sample:
            qt, k_new, vt_new = _proj_call(st, x, mod4, 1, *proj_head, w_uq.T, *proj_tail, w_uvt,
                                           rope_tables=_axial_rope_tables(st.seq))
            past = cache_ckv.shape[2]
            krz = jnp.pad(cache_krope[:, 0].reshape(st.batch * past, ROPE_DIM), ((0, 0), (0, ROPE_DIM)))
            k_cache, vt_cache = _cachekv_call(
                cache_ckv[:, 0].reshape(st.batch * past, KV_LORA), krz, w_uk, w_uvt)
            attn = _attn_pipe_call(st.batch, st.seq, qt, k_new, vt_new, k_cache, vt_cache)
        else:
            q, k_new, v_new, ckv, kr = _proj_call(st, x, mod4, 1, *proj_head, w_uq, *proj_tail, w_uv)
            attn = _attn_call(st.batch, st.seq, q, k_new, v_new)
            outs += [ckv.reshape(st.batch, 1, st.seq, KV_LORA), kr.reshape(st.batch, 1, st.seq, ROPE_DIM)]
        y = _ffn_call(st, x, mod4, 1, row(norm2_g[1]), f_w_in_b[1], f_w_out_b[1],
                      attn=attn, w_o=w_o, final_g=row(final_g))
        outs.append(y.reshape(x_in.shape))

    new_ckv, new_krope, y_prompt, y_sample = outs
    return (y_prompt, y_sample, new_ckv, new_krope)
```

```python
import functools
import math
from typing import NamedTuple

import jax
import jax.numpy as jnp
from jax import lax
from jax.experimental import pallas as pl
from jax.experimental.pallas import tpu as pltpu

F32 = jnp.float32
BF16 = jnp.bfloat16

D_MODEL = 1024
EPS = 1e-6
CHUNK = 128
A_WIDTH = 2 * D_MODEL
A_GROUPS = 8
A_GROUP_DIM = A_WIDTH // A_GROUPS
N_HEADS = 8
Q_LORA = 512
KV_LORA = 256
NOPE_DIM = 128
ROPE_DIM = 64
V_DIM = 128
ROPE_AXIS_PAIRS = ROPE_DIM // 4
ROPE_BASE = 10000.0
GRID_W = 64
ATTN_SCALE = 1.0 / math.sqrt(NOPE_DIM + ROPE_DIM)
Q_SCALE = ATTN_SCALE * math.log2(math.e)
D_FF = ((8 * D_MODEL // 3 + 255) // 256) * 256
SQRT_HALF = math.sqrt(0.5)

HEAD_W = NOPE_DIM + 2 * ROPE_DIM
MOD_ROWS = 16
MOD_TN = 1536
TOKEN_TILE = 512
SUB_TILE = 256
ATTN_SUB_ROWS = 256
ATTN_KEY_CHUNK = 512
ATTN_PIPE_HEADS = 4
ATTN_BATCH_PER_STEP = 8
VMEM_LIMIT_BYTES = 56 * 1024 * 1024


def _rms(x, g):
    return x * lax.rsqrt(jnp.mean(x * x, axis=-1, keepdims=True) + EPS) * g


def _norm_modulate(x, g, mod, shift_chunk, scale_chunk):
    gain = g * (1.0 + _mod_chunk(mod, scale_chunk))
    inv = lax.rsqrt(jnp.mean(x * x, axis=-1, keepdims=True) + EPS)
    return x * inv * gain + _mod_chunk(mod, shift_chunk)


def _gelu(z):
    return 0.5 * z * (1.0 + lax.erf(z * SQRT_HALF))


def _mod_chunk(mod, k):
    return mod[:, k * D_MODEL:(k + 1) * D_MODEL]


def _mod_kernel(c_ref, w_ref, b_ref, o_ref):
    s = jax.nn.silu(c_ref[...]).astype(BF16)
    o_ref[...] = jnp.dot(s, w_ref[...].astype(BF16), preferred_element_type=F32) + b_ref[...]


def _mod_call(cond, ada_w, ada_b):
    depth = ada_w.shape[0]
    n = ada_w.shape[2]
    return pl.pallas_call(
        _mod_kernel,
        grid=(depth, n // MOD_TN),
        in_specs=[
            pl.BlockSpec((MOD_ROWS, D_MODEL), lambda i, j: (0, 0)),
            pl.BlockSpec((None, D_MODEL, MOD_TN), lambda i, j: (i, 0, j)),
            pl.BlockSpec((None, 1, MOD_TN), lambda i, j: (i, 0, j)),
        ],
        out_specs=pl.BlockSpec((None, MOD_ROWS, MOD_TN), lambda i, j: (i, 0, j)),
        out_shape=jax.ShapeDtypeStruct((depth, MOD_ROWS, n), F32),
        compiler_params=pltpu.CompilerParams(
            dimension_semantics=("arbitrary", "arbitrary"), vmem_limit_bytes=VMEM_LIMIT_BYTES),
        name="mod",
    )(cond, ada_w, ada_b.reshape(depth, 1, n))


def _gmlp_kernel(x_ref, mod_ref, n1_ref, win_ref, vg_ref, ws_ref, bs_ref, wout_ref, o_ref, y_ref):
    mod = mod_ref[...]
    units = [slice(r, r + SUB_TILE) for r in range(0, x_ref.shape[0], SUB_TILE)]
    heads = []
    for rows in units:
        x = x_ref[rows, :]
        heads.append((x, _norm_modulate(x, n1_ref[...], mod, 0, 1).astype(BF16)))
    for rows, (x, hb) in zip(units, heads):
        v = jnp.dot(hb, win_ref[:, A_WIDTH:], preferred_element_type=F32)
        v = _rms(_gelu(v), vg_ref[...]).astype(BF16)
        u = _gelu(jnp.dot(hb, win_ref[:, :A_WIDTH], preferred_element_type=F32))
        for c in range(SUB_TILE // CHUNK):
            crows = slice(c * CHUNK, (c + 1) * CHUNK)
            yrows = slice(rows.start + c * CHUNK, rows.start + (c + 1) * CHUNK)
            for g in range(A_GROUPS):
                cols = slice(g * A_GROUP_DIM, (g + 1) * A_GROUP_DIM)
                s = jnp.dot(ws_ref[g], v[crows, cols], preferred_element_type=F32) + bs_ref[g]
                y_ref[yrows, cols] = (u[crows, cols] * s).astype(BF16)
        out = jnp.dot(y_ref[rows, :], wout_ref[...], preferred_element_type=F32)
        o_ref[rows, :] = x + _mod_chunk(mod, 2) * out


def _ffn_kernel(*refs, has_attn, final):
    refs = list(refs)
    x_ref, mod_ref = refs[:2]
    pos = 2
    if has_attn:
        a_ref, wo_ref = refs[pos:pos + 2]
        pos += 2
    n2_ref, win_ref, wout_ref = refs[pos:pos + 3]
    pos += 3
    if final:
        fg_ref = refs[pos]
        pos += 1
    o_ref = refs[pos]

    mod = mod_ref[...]
    units = [slice(r, r + SUB_TILE) for r in range(0, x_ref.shape[0], SUB_TILE)]
    heads = []
    for rows in units:
        x = x_ref[rows, :]
        if has_attn:
            x = x + _mod_chunk(mod, 2) * jnp.dot(a_ref[rows, :], wo_ref[...], preferred_element_type=F32)
        heads.append((x, _norm_modulate(x, n2_ref[...], mod, 3, 4).astype(BF16)))
    for rows, (x, h) in zip(units, heads):
        gu = jnp.dot(h, win_ref[...], preferred_element_type=F32)
        a = (jax.nn.silu(gu[:, :D_FF]) * gu[:, D_FF:]).astype(BF16)
        y = x + _mod_chunk(mod, 5) * jnp.dot(a, wout_ref[...], preferred_element_type=F32)
        if final:
            y = _rms(y, fg_ref[...])
        o_ref[rows, :] = y


_NT = (((1,), (1,)), ((), ()))


def _store_kv(ckv_bf16, krz_bf16, wuk_ref, wuv_ref, k_ref, v_ref, pipe_layout):
    kn = jnp.dot(ckv_bf16, wuk_ref[...], preferred_element_type=F32)
    for hh in range(N_HEADS):
        kn_h = kn[:, hh * NOPE_DIM:(hh + 1) * NOPE_DIM].astype(BF16)
        if pipe_layout:
            k_ref[hh, :, :NOPE_DIM] = kn_h
            k_ref[hh, :, NOPE_DIM:] = krz_bf16
        else:
            k_ref[:, hh * HEAD_W:hh * HEAD_W + NOPE_DIM] = kn_h
            k_ref[:, hh * HEAD_W + NOPE_DIM:(hh + 1) * HEAD_W] = krz_bf16
    if pipe_layout:
        v = lax.dot_general(wuv_ref[...], ckv_bf16, _NT, preferred_element_type=F32)
    else:
        v = jnp.dot(ckv_bf16, wuv_ref[...], preferred_element_type=F32)
    v_ref[...] = v.astype(BF16)


def _proj_kernel(*refs, rope):
    refs = list(refs)
    x_ref, mod_ref, n1_ref, wdq_ref, qg_ref, wuq_ref, wdkv_ref, kvg_ref, wuk_ref, wuv_ref = refs[:10]
    pos = 10
    if rope:
        cos_ref, sin_ref, cost_ref, sint_ref = refs[pos:pos + 4]
        pos += 4
    q_ref, k_ref, v_ref = refs[pos:pos + 3]
    pos += 3
    if not rope:
        ckv_ref, kr_ref = refs[pos:pos + 2]

    x = x_ref[...]
    mod = mod_ref[...]
    h = _norm_modulate(x, n1_ref[...], mod, 0, 1).astype(BF16)
    ql = _rms(jnp.dot(h, wdq_ref[...], preferred_element_type=F32), qg_ref[...]).astype(BF16)
    kv = jnp.dot(h, wdkv_ref[...], preferred_element_type=F32)
    ckv = _rms(kv[:, :KV_LORA], kvg_ref[...])
    krb = kv[:, KV_LORA:]

    if rope:
        cs, sn = cos_ref[...], sin_ref[...]
        cst, snt = cost_ref[...], sint_ref[...]
        krz = krb * cs + pltpu.roll(krb, ROPE_DIM, 1) * sn
        qt = lax.dot_general(wuq_ref[...], ql, _NT, preferred_element_type=F32)
        for hh in range(N_HEADS):
            lo = hh * HEAD_W
            q_ref[lo:lo + NOPE_DIM, :] = (qt[lo:lo + NOPE_DIM, :] * Q_SCALE).astype(BF16)
            b = qt[lo + NOPE_DIM:lo + HEAD_W, :]
            roped = b * cst + pltpu.roll(b, ROPE_DIM, 0) * snt
            q_ref[lo + NOPE_DIM:lo + HEAD_W, :] = (roped * Q_SCALE).astype(BF16)
    else:
        keep = (lax.broadcasted_iota(jnp.int32, (1, 2 * ROPE_DIM), 1) < ROPE_DIM).astype(F32)
        krz = krb * keep
        q = jnp.dot(ql, wuq_ref[...], preferred_element_type=F32)
        for hh in range(N_HEADS):
            lo = hh * HEAD_W
            q_ref[:, lo:lo + NOPE_DIM] = (q[:, lo:lo + NOPE_DIM] * Q_SCALE).astype(BF16)
            q_ref[:, lo + NOPE_DIM:lo + HEAD_W] = (q[:, lo + NOPE_DIM:lo + HEAD_W] * (keep * Q_SCALE)).astype(BF16)
    _store_kv(ckv.astype(BF16), krz.astype(BF16), wuk_ref, wuv_ref, k_ref, v_ref, pipe_layout=rope)
    if not rope:
        ckv_ref[...] = ckv
        kr_ref[...] = krb[:, :ROPE_DIM]


def _cachekv_kernel(ckv_ref, krz_ref, wuk_ref, wuvt_ref, k_ref, vt_ref):
    _store_kv(ckv_ref[...].astype(BF16), krz_ref[...].astype(BF16), wuk_ref, wuvt_ref, k_ref, vt_ref,
              pipe_layout=True)


def _attn_kernel(q_ref, k_ref, v_ref, o_ref, *, seq):
    for b in range(q_ref.shape[0] // seq):
        rows = slice(b * seq, (b + 1) * seq)
        for hh in range(N_HEADS):
            kcols = slice(hh * HEAD_W, (hh + 1) * HEAD_W)
            vcols = slice(hh * V_DIM, (hh + 1) * V_DIM)
            s = lax.dot_general(q_ref[rows, kcols], k_ref[rows, kcols], _NT, preferred_element_type=F32)
            p = jnp.exp2(s - jnp.max(s, axis=-1, keepdims=True))
            l = jnp.sum(p, axis=-1, keepdims=True)
            acc = jnp.dot(p.astype(BF16), v_ref[rows, vcols], preferred_element_type=F32)
            o_ref[rows, vcols] = (acc / l).astype(BF16)


def _fold8(x, op):
    return op(x.reshape(x.shape[0] // 8, 8, x.shape[1]), axis=0)


def _attn_pipe_kernel(qt_ref, kn_ref, vtn_ref, kc_ref, vtc_ref, o_ref, st0_ref, st1_ref, m0_ref, m1_ref):
    sources = ((kn_ref, vtn_ref), (kc_ref, vtc_ref))
    heads = qt_ref.shape[0] // HEAD_W
    units = [(hh, r) for hh in range(heads) for r in range(qt_ref.shape[1] // ATTN_SUB_ROWS)]
    bufs = ((st0_ref, m0_ref), (st1_ref, m1_ref))

    def chunks():
        off = 0
        for k_ref, vt_ref in sources:
            for c in range(k_ref.shape[1] // ATTN_KEY_CHUNK):
                keys = slice(c * ATTN_KEY_CHUNK, (c + 1) * ATTN_KEY_CHUNK)
                yield k_ref, vt_ref, keys, slice(off, off + ATTN_KEY_CHUNK)
                off += ATTN_KEY_CHUNK

    def sub_rows(r):
        return slice(r * ATTN_SUB_ROWS, (r + 1) * ATTN_SUB_ROWS)

    def score_phase(i):
        hh, r = units[i]
        st_ref, m_ref = bufs[i % 2]
        qt = qt_ref[hh * HEAD_W:(hh + 1) * HEAD_W, sub_rows(r)]
        m8 = None
        for k_ref, _, keys, srows in chunks():
            st = jnp.dot(k_ref[hh, keys, :], qt, preferred_element_type=F32)
            st_ref[srows, :] = st
            part = _fold8(st, jnp.max)
            m8 = part if m8 is None else jnp.maximum(m8, part)
        m_ref[...] = jnp.max(m8, axis=0, keepdims=True)

    def value_phase(i):
        hh, r = units[i]
        st_ref, m_ref = bufs[i % 2]
        vrows = slice(hh * V_DIM, (hh + 1) * V_DIM)
        m = m_ref[...]
        l8 = None
        acc = None
        for _, vt_ref, keys, srows in chunks():
            pt = jnp.exp2(st_ref[srows, :] - m)
            part = _fold8(pt, jnp.sum)
            pv = jnp.dot(vt_ref[vrows, keys], pt.astype(BF16), preferred_element_type=F32)
            l8 = part if l8 is None else l8 + part
            acc = pv if acc is None else acc + pv
        l = jnp.sum(l8, axis=0, keepdims=True)
        o_ref[sub_rows(r), vrows] = (acc / l).T.astype(BF16)

    score_phase(0)
    for i in range(1, len(units)):
        score_phase(i)
        value_phase(i - 1)
    value_phase(len(units) - 1)


class _Layer(NamedTuple):
    stacked: jax.Array
    layer: int


def _const_spec(const):
    if isinstance(const, _Layer):
        shape = const.stacked.shape[1:]
        index = (const.layer,) + (0,) * len(shape)
        return pl.BlockSpec((None,) + shape, lambda *_: index, pipeline_mode=pl.Buffered(1))
    zeros = (0,) * const.ndim
    return pl.BlockSpec(const.shape, lambda *_: zeros, pipeline_mode=pl.Buffered(1))


def _const_arg(const):
    return const.stacked if isinstance(const, _Layer) else const


def _token_params():
    return pltpu.CompilerParams(dimension_semantics=("arbitrary",), vmem_limit_bytes=VMEM_LIMIT_BYTES)


def _rotate_half_cols(w):
    ws = w.reshape(w.shape[:-1] + (2, 2, ROPE_AXIS_PAIRS))
    return jnp.concatenate([-ws[..., 1:, :], ws[..., :1, :]], axis=-2).reshape(w.shape)


def _axial_rope_tables(n_tokens):
    rows = n_tokens // GRID_W
    row = jnp.repeat(jnp.arange(rows), GRID_W)
    col = jnp.tile(jnp.arange(GRID_W), rows)
    inv = 1.0 / (ROPE_BASE ** (jnp.arange(ROPE_AXIS_PAIRS, dtype=F32) / ROPE_AXIS_PAIRS))
    ang = jnp.stack([row, col], axis=-1).astype(F32)[:, :, None] * inv
    ang = jnp.broadcast_to(ang[:, :, None, :], (rows * GRID_W, 2, 2, ROPE_AXIS_PAIRS))
    ang = ang.reshape(rows * GRID_W, ROPE_DIM)
    pad = ((0, 0), (0, ROPE_DIM))
    return jnp.pad(jnp.cos(ang), pad), jnp.pad(jnp.sin(ang), pad)


class _Stream:
    def __init__(self, x, first_mod_row, shared_mod):
        self.batch, self.seq, _ = x.shape
        self.x = x.reshape(self.batch * self.seq, D_MODEL)
        self.tokens = self.batch * self.seq
        self.tiles = self.tokens // TOKEN_TILE
        tiles_per_batch = self.seq // TOKEN_TILE
        if shared_mod:
            self.mod_row = lambda t: first_mod_row
        else:
            self.mod_row = lambda t: first_mod_row + t // tiles_per_batch
        self.tiles_per_batch = tiles_per_batch

    def tile_spec(self, width):
        return pl.BlockSpec((TOKEN_TILE, width), lambda t: (t, 0))

    def mod_spec(self, layer):
        return pl.BlockSpec((None, None, 1, 6 * D_MODEL), lambda t: (layer, self.mod_row(t), 0, 0))


def _gmlp_call(st, x, mod4, layer, n1, w_in, v_g, w_s, b_s, w_out):
    consts = (n1, w_in, v_g, w_s, b_s, w_out)
    return pl.pallas_call(
        _gmlp_kernel,
        grid=(st.tiles,),
        in_specs=[st.tile_spec(D_MODEL), st.mod_spec(layer)] + [_const_spec(a) for a in consts],
        out_specs=st.tile_spec(D_MODEL),
        out_shape=jax.ShapeDtypeStruct((st.tokens, D_MODEL), F32),
        scratch_shapes=[pltpu.VMEM((TOKEN_TILE, A_WIDTH), BF16)],
        compiler_params=_token_params(),
        name="gmlp",
    )(x, mod4, *map(_const_arg, consts))


def _ffn_call(st, x, mod4, layer, n2, w_in, w_out, attn=None, w_o=None, final_g=None):
    has_attn = attn is not None
    final = final_g is not None
    args = [x, mod4]
    specs = [st.tile_spec(D_MODEL), st.mod_spec(layer)]
    if has_attn:
        args += [attn, _const_arg(w_o)]
        specs += [st.tile_spec(N_HEADS * V_DIM), _const_spec(w_o)]
    consts = [n2, w_in, w_out] + ([final_g] if final else [])
    args += map(_const_arg, consts)
    specs += [_const_spec(a) for a in consts]
    return pl.pallas_call(
        functools.partial(_ffn_kernel, has_attn=has_attn, final=final),
        grid=(st.tiles,),
        in_specs=specs,
        out_specs=st.tile_spec(D_MODEL),
        out_shape=jax.ShapeDtypeStruct((st.tokens, D_MODEL), F32),
        compiler_params=_token_params(),
        name="ffn",
    )(*args)


def _transposed_tile_spec(features):
    return pl.BlockSpec((features, TOKEN_TILE), lambda t: (0, t))


def _head_major_tile_spec():
    return pl.BlockSpec((N_HEADS, TOKEN_TILE, HEAD_W), lambda t: (0, t, 0))


def _proj_call(st, x, mod4, layer, n1, w_dq, q_g, w_uq, w_dkv, kv_g, w_uk, w_uv, rope_tables=None):
    rope = rope_tables is not None
    consts = (n1, w_dq, q_g, w_uq, w_dkv, kv_g, w_uk, w_uv)
    args = [x, mod4, *map(_const_arg, consts)]
    specs = [st.tile_spec(D_MODEL), st.mod_spec(layer)] + [_const_spec(a) for a in consts]
    qw, vw = N_HEADS * HEAD_W, N_HEADS * V_DIM
    out_shape = [
        jax.ShapeDtypeStruct((qw, st.tokens) if rope else (st.tokens, qw), BF16),
        jax.ShapeDtypeStruct((N_HEADS, st.tokens, HEAD_W) if rope else (st.tokens, qw), BF16),
        jax.ShapeDtypeStruct((vw, st.tokens) if rope else (st.tokens, vw), BF16),
    ]
    out_specs = [_transposed_tile_spec(qw) if rope else st.tile_spec(qw),
                 _head_major_tile_spec() if rope else st.tile_spec(qw),
                 _transposed_tile_spec(vw) if rope else st.tile_spec(vw)]
    if rope:
        tpb = st.tiles_per_batch
        cos, sin = rope_tables
        table_spec = pl.BlockSpec((TOKEN_TILE, 2 * ROPE_DIM), lambda t: (t % tpb, 0))
        table_t_spec = pl.BlockSpec((2 * ROPE_DIM, TOKEN_TILE), lambda t: (0, t % tpb))
        args += [cos, sin, cos.T, sin.T]
        specs += [table_spec, table_spec, table_t_spec, table_t_spec]
    else:
        out_shape += [jax.ShapeDtypeStruct((st.tokens, KV_LORA), F32),
                      jax.ShapeDtypeStruct((st.tokens, ROPE_DIM), F32)]
        out_specs += [st.tile_spec(KV_LORA), st.tile_spec(ROPE_DIM)]
    return pl.pallas_call(
        functools.partial(_proj_kernel, rope=rope),
        grid=(st.tiles,),
        in_specs=specs,
        out_specs=out_specs,
        out_shape=out_shape,
        compiler_params=_token_params(),
        name="proj",
    )(*args)


def _cachekv_call(ckv, krz, w_uk, w_uvt):
    tokens = ckv.shape[0]
    tile = lambda width: pl.BlockSpec((TOKEN_TILE, width), lambda t: (t, 0))
    return pl.pallas_call(
        _cachekv_kernel,
        grid=(tokens // TOKEN_TILE,),
        in_specs=[tile(KV_LORA), tile(2 * ROPE_DIM), _const_spec(w_uk), _const_spec(w_uvt)],
        out_specs=[_head_major_tile_spec(), _transposed_tile_spec(N_HEADS * V_DIM)],
        out_shape=[jax.ShapeDtypeStruct((N_HEADS, tokens, HEAD_W), BF16),
                   jax.ShapeDtypeStruct((N_HEADS * V_DIM, tokens), BF16)],
        compiler_params=_token_params(),
        name="cachekv",
    )(ckv, krz, w_uk, w_uvt)


def _attn_call(batch, seq, q, k, v):
    rows = ATTN_BATCH_PER_STEP * seq
    spec = lambda width: pl.BlockSpec((rows, width), lambda b: (b, 0))
    return pl.pallas_call(
        functools.partial(_attn_kernel, seq=seq),
        grid=(batch // ATTN_BATCH_PER_STEP,),
        in_specs=[spec(N_HEADS * HEAD_W), spec(N_HEADS * HEAD_W), spec(N_HEADS * V_DIM)],
        out_specs=spec(N_HEADS * V_DIM),
        out_shape=jax.ShapeDtypeStruct((batch * seq, N_HEADS * V_DIM), BF16),
        compiler_params=_token_params(),
        name="attn",
    )(q, k, v)


def _attn_pipe_call(batch, seq, q, k_new, vt_new, k_cache, vt_cache):
    past = k_cache.shape[1] // batch
    kw, vw = ATTN_PIPE_HEADS * HEAD_W, ATTN_PIPE_HEADS * V_DIM
    return pl.pallas_call(
        _attn_pipe_kernel,
        grid=(batch, N_HEADS // ATTN_PIPE_HEADS),
        in_specs=[
            pl.BlockSpec((kw, seq), lambda b, h: (h, b)),
            pl.BlockSpec((ATTN_PIPE_HEADS, seq, HEAD_W), lambda b, h: (h, b, 0)),
            pl.BlockSpec((vw, seq), lambda b, h: (h, b)),
            pl.BlockSpec((ATTN_PIPE_HEADS, past, HEAD_W), lambda b, h: (h, b, 0)),
            pl.BlockSpec((vw, past), lambda b, h: (h, b)),
        ],
        out_specs=pl.BlockSpec((seq, vw), lambda b, h: (b, h)),
        out_shape=jax.ShapeDtypeStruct((batch * seq, N_HEADS * V_DIM), BF16),
        scratch_shapes=[pltpu.VMEM((seq + past, ATTN_SUB_ROWS), F32)] * 2
        + [pltpu.VMEM((1, ATTN_SUB_ROWS), F32)] * 2,
        compiler_params=pltpu.CompilerParams(
            dimension_semantics=("arbitrary", "arbitrary"), vmem_limit_bytes=VMEM_LIMIT_BYTES),
        name="attn_pipe",
    )(q, k_new, vt_new, k_cache, vt_cache)


def kernel(x_prompt, x_sample, cache_ckv, cache_krope, c, c_ctx, ada_w, ada_b, norm1_g, norm2_g, a_w_in, a_v_g, a_w_s, a_b_s, a_w_out, b_w_dq, b_q_g, b_w_uq, b_w_dkv, b_kv_g, b_w_ukv, b_w_o, f_w_in, f_w_out, final_g):
    dec_batch = x_sample.shape[0]
    row = lambda g: g.reshape(1, -1)

    cond = jnp.concatenate(
        [c_ctx[None, :], c, jnp.zeros((MOD_ROWS - 1 - dec_batch, D_MODEL), F32)], axis=0)
    mod = _mod_call(cond, ada_w, ada_b)
    mod4 = mod.reshape(mod.shape[0], MOD_ROWS, 1, 6 * D_MODEL)

    a_w_in_b = _Layer(a_w_in.astype(BF16), 0)
    a_w_s_b = _Layer(a_w_s.astype(BF16), 0)
    a_b_s_c = a_b_s[0][:, :, None]
    a_w_out_b = _Layer(a_w_out.astype(BF16), 0)
    f_w_in_b = f_w_in.astype(BF16)
    f_w_out_b = f_w_out.astype(BF16)
    w_dq = _Layer(b_w_dq.astype(BF16), 0)
    uq = b_w_uq[0].reshape(Q_LORA, N_HEADS, NOPE_DIM + ROPE_DIM)
    uq_rope = uq[..., NOPE_DIM:]
    w_uq = jnp.concatenate([uq[..., :NOPE_DIM], uq_rope, _rotate_half_cols(uq_rope)], axis=-1)
    w_uq = w_uq.reshape(Q_LORA, N_HEADS * HEAD_W).astype(BF16)
    dkv_rope = b_w_dkv[0][:, KV_LORA:]
    w_dkv = jnp.concatenate([b_w_dkv[0], _rotate_half_cols(dkv_rope)], axis=-1).astype(BF16)
    ukv = b_w_ukv[0].reshape(KV_LORA, N_HEADS, NOPE_DIM + V_DIM)
    w_uk = ukv[..., :NOPE_DIM].reshape(KV_LORA, N_HEADS * NOPE_DIM).astype(BF16)
    w_uv = ukv[..., NOPE_DIM:].reshape(KV_LORA, N_HEADS * V_DIM).astype(BF16)
    w_uvt = w_uv.T
    w_o = _Layer(b_w_o.astype(BF16), 0)

    outs = []
    for x_in, first_row, shared in ((x_prompt, 0, True), (x_sample, 1, False)):
        st = _Stream(x_in, first_row, shared)
        is_sample = not shared
        x = _gmlp_call(st, st.x, mod4, 0, row(norm1_g[0]), a_w_in_b, row(a_v_g[0]), a_w_s_b, a_b_s_c, a_w_out_b)
        x = _ffn_call(st, x, mod4, 0, row(norm2_g[0]), _Layer(f_w_in_b, 0), _Layer(f_w_out_b, 0))
        proj_head = (row(norm1_g[1]), w_dq, row(b_q_g[0]))
        proj_tail = (w_dkv, row(b_kv_g[0]), w_uk)
        if is_sample:
            qt, k_new, vt_new = _proj_call(st, x, mod4, 1, *proj_head, w_uq.T, *proj_tail, w_uvt,
                                           rope_tables=_axial_rope_tables(st.seq))
            past = cache_ckv.shape[2]
            krz = jnp.pad(cache_krope[:, 0].reshape(st.batch * past, ROPE_DIM), ((0, 0), (0, ROPE_DIM)))
            k_cache, vt_cache = _cachekv_call(
                cache_ckv[:, 0].reshape(st.batch * past, KV_LORA), krz, w_uk, w_uvt)
            attn = _attn_pipe_call(st.batch, st.seq, qt, k_new, vt_new, k_cache, vt_cache)
        else:
            q, k_new, v_new, ckv, kr = _proj_call(st, x, mod4, 1, *proj_head, w_uq, *proj_tail, w_uv)
            attn = _attn_call(st.batch, st.seq, q, k_new, v_new)
            outs += [ckv.reshape(st.batch, 1, st.seq, KV_LORA), kr.reshape(st.batch, 1, st.seq, ROPE_DIM)]
        y = _ffn_call(st, x, mod4, 1, row(norm2_g[1]), _Layer(f_w_in_b, 1), _Layer(f_w_out_b, 1),
                      attn=attn, w_o=w_o, final_g=row(final_g))
        outs.append(y.reshape(x_in.shape))

    new_ckv, new_krope, y_prompt, y_sample = outs
    return (y_prompt, y_sample, new_ckv, new_krope)
```

```python
import functools
import math
from typing import NamedTuple

import jax
import jax.numpy as jnp
from jax import lax
from jax.experimental import pallas as pl
from jax.experimental.pallas import tpu as pltpu

F32 = jnp.float32
BF16 = jnp.bfloat16

D_MODEL = 1024
EPS = 1e-6
CHUNK = 128
A_WIDTH = 2 * D_MODEL
A_GROUPS = 8
A_GROUP_DIM = A_WIDTH // A_GROUPS
N_HEADS = 8
Q_LORA = 512
KV_LORA = 256
NOPE_DIM = 128
ROPE_DIM = 64
V_DIM = 128
ROPE_AXIS_PAIRS = ROPE_DIM // 4
ROPE_BASE = 10000.0
GRID_W = 64
ATTN_SCALE = 1.0 / math.sqrt(NOPE_DIM + ROPE_DIM)
Q_SCALE = ATTN_SCALE * math.log2(math.e)
D_FF = ((8 * D_MODEL // 3 + 255) // 256) * 256
SQRT_HALF = math.sqrt(0.5)

HEAD_W = NOPE_DIM + 2 * ROPE_DIM
MOD_ROWS = 16
MOD_TN = 1536
TOKEN_TILE = 512
SUB_TILE = 256
ATTN_SUB_ROWS = 256
ATTN_KEY_CHUNK = 512
ATTN_PIPE_HEADS = 4
ATTN_BATCH_PER_STEP = 8
VMEM_LIMIT_BYTES = 56 * 1024 * 1024


def _rms(x, g):
    return x * lax.rsqrt(jnp.mean(x * x, axis=-1, keepdims=True) + EPS) * g


def _norm_modulate(x, g, mod, shift_chunk, scale_chunk):
    gain = g * (1.0 + _mod_chunk(mod, scale_chunk))
    inv = lax.rsqrt(jnp.mean(x * x, axis=-1, keepdims=True) + EPS)
    return x * inv * gain + _mod_chunk(mod, shift_chunk)


def _gelu(z):
    return 0.5 * z * (1.0 + lax.erf(z * SQRT_HALF))


def _mod_chunk(mod, k):
    return mod[:, k * D_MODEL:(k + 1) * D_MODEL]


def _mod_kernel(c_ref, w_ref, b_ref, o_ref):
    s = jax.nn.silu(c_ref[...]).astype(BF16)
    o_ref[...] = jnp.dot(s, w_ref[...].astype(BF16), preferred_element_type=F32) + b_ref[...]


def _mod_call(cond, ada_w, ada_b):
    depth = ada_w.shape[0]
    n = ada_w.shape[2]
    return pl.pallas_call(
        _mod_kernel,
        grid=(depth, n // MOD_TN),
        in_specs=[
            pl.BlockSpec((MOD_ROWS, D_MODEL), lambda i, j: (0, 0)),
            pl.BlockSpec((None, D_MODEL, MOD_TN), lambda i, j: (i, 0, j)),
            pl.BlockSpec((None, 1, MOD_TN), lambda i, j: (i, 0, j)),
        ],
        out_specs=pl.BlockSpec((None, MOD_ROWS, MOD_TN), lambda i, j: (i, 0, j)),
        out_shape=jax.ShapeDtypeStruct((depth, MOD_ROWS, n), F32),
        compiler_params=pltpu.CompilerParams(
            dimension_semantics=("arbitrary", "arbitrary"), vmem_limit_bytes=VMEM_LIMIT_BYTES),
        name="mod",
    )(cond, ada_w, ada_b.reshape(depth, 1, n))


def _gmlp_kernel(x_ref, mod_ref, n1_ref, win_ref, vg_ref, ws_ref, bs_ref, wout_ref, o_ref, y_ref):
    x = x_ref[...]
    mod = mod_ref[...]
    hb = _norm_modulate(x, n1_ref[...], mod, 0, 1).astype(BF16)
    v = jnp.dot(hb, win_ref[:, A_WIDTH:], preferred_element_type=F32)
    v = _rms(_gelu(v), vg_ref[...]).astype(BF16)
    u = _gelu(jnp.dot(hb, win_ref[:, :A_WIDTH], preferred_element_type=F32))
    for c in range(x.shape[0] // CHUNK):
        rows = slice(c * CHUNK, (c + 1) * CHUNK)
        for g in range(A_GROUPS):
            cols = slice(g * A_GROUP_DIM, (g + 1) * A_GROUP_DIM)
            s = jnp.dot(ws_ref[g], v[rows, cols], preferred_element_type=F32) + bs_ref[g]
            y_ref[rows, cols] = (u[rows, cols] * s).astype(BF16)
    out = jnp.dot(y_ref[...], wout_ref[...], preferred_element_type=F32)
    o_ref[...] = x + _mod_chunk(mod, 2) * out


def _ffn_kernel(*refs, has_attn, final):
    refs = list(refs)
    x_ref, mod_ref = refs[:2]
    pos = 2
    if has_attn:
        a_ref, wo_ref = refs[pos:pos + 2]
        pos += 2
    n2_ref, win_ref, wout_ref = refs[pos:pos + 3]
    pos += 3
    if final:
        fg_ref = refs[pos]
        pos += 1
    o_ref = refs[pos]

    mod = mod_ref[...]
    units = [slice(r, r + SUB_TILE) for r in range(0, x_ref.shape[0], SUB_TILE)]
    heads = []
    for rows in units:
        x = x_ref[rows, :]
        if has_attn:
            x = x + _mod_chunk(mod, 2) * jnp.dot(a_ref[rows, :], wo_ref[...], preferred_element_type=F32)
        heads.append((x, _norm_modulate(x, n2_ref[...], mod, 3, 4).astype(BF16)))
    for rows, (x, h) in zip(units, heads):
        gu = jnp.dot(h, win_ref[...], preferred_element_type=F32)
        a = (jax.nn.silu(gu[:, :D_FF]) * gu[:, D_FF:]).astype(BF16)
        y = x + _mod_chunk(mod, 5) * jnp.dot(a, wout_ref[...], preferred_element_type=F32)
        if final:
            y = _rms(y, fg_ref[...])
        o_ref[rows, :] = y


_NT = (((1,), (1,)), ((), ()))


def _store_kv(ckv_bf16, krz_bf16, wuk_ref, wuv_ref, k_ref, v_ref, toks, pipe_layout):
    kn = jnp.dot(ckv_bf16, wuk_ref[...], preferred_element_type=F32)
    for hh in range(N_HEADS):
        kn_h = kn[:, hh * NOPE_DIM:(hh + 1) * NOPE_DIM].astype(BF16)
        if pipe_layout:
            k_ref[hh, toks, :NOPE_DIM] = kn_h
            k_ref[hh, toks, NOPE_DIM:] = krz_bf16
        else:
            k_ref[toks, hh * HEAD_W:hh * HEAD_W + NOPE_DIM] = kn_h
            k_ref[toks, hh * HEAD_W + NOPE_DIM:(hh + 1) * HEAD_W] = krz_bf16
    if pipe_layout:
        v_ref[:, toks] = lax.dot_general(wuv_ref[...], ckv_bf16, _NT, preferred_element_type=F32).astype(BF16)
    else:
        v_ref[toks, :] = jnp.dot(ckv_bf16, wuv_ref[...], preferred_element_type=F32).astype(BF16)


def _proj_kernel(*refs, rope):
    refs = list(refs)
    x_ref, mod_ref, n1_ref, wdq_ref, qg_ref, wuq_ref, wdkv_ref, kvg_ref, wuk_ref, wuv_ref = refs[:10]
    pos = 10
    if rope:
        cos_ref, sin_ref, cost_ref, sint_ref = refs[pos:pos + 4]
        pos += 4
    q_ref, k_ref, v_ref = refs[pos:pos + 3]
    pos += 3
    if not rope:
        ckv_ref, kr_ref = refs[pos:pos + 2]

    mod = mod_ref[...]
    units = [slice(r, r + SUB_TILE) for r in range(0, x_ref.shape[0], SUB_TILE)]
    heads = [_norm_modulate(x_ref[toks, :], n1_ref[...], mod, 0, 1).astype(BF16) for toks in units]

    for toks, h in zip(units, heads):
        ql = _rms(jnp.dot(h, wdq_ref[...], preferred_element_type=F32), qg_ref[...]).astype(BF16)
        kv = jnp.dot(h, wdkv_ref[...], preferred_element_type=F32)
        ckv = _rms(kv[:, :KV_LORA], kvg_ref[...])
        krb = kv[:, KV_LORA:]
        if rope:
            cs, sn = cos_ref[toks, :], sin_ref[toks, :]
            cst, snt = cost_ref[:, toks], sint_ref[:, toks]
            krz = krb * cs + pltpu.roll(krb, ROPE_DIM, 1) * sn
            qt = lax.dot_general(wuq_ref[...], ql, _NT, preferred_element_type=F32)
            for hh in range(N_HEADS):
                lo = hh * HEAD_W
                q_ref[lo:lo + NOPE_DIM, toks] = (qt[lo:lo + NOPE_DIM, :] * Q_SCALE).astype(BF16)
                b = qt[lo + NOPE_DIM:lo + HEAD_W, :]
                roped = b * cst + pltpu.roll(b, ROPE_DIM, 0) * snt
                q_ref[lo + NOPE_DIM:lo + HEAD_W, toks] = (roped * Q_SCALE).astype(BF16)
        else:
            keep = (lax.broadcasted_iota(jnp.int32, (1, 2 * ROPE_DIM), 1) < ROPE_DIM).astype(F32)
            krz = krb * keep
            q = jnp.dot(ql, wuq_ref[...], preferred_element_type=F32)
            for hh in range(N_HEADS):
                lo = hh * HEAD_W
                q_ref[toks, lo:lo + NOPE_DIM] = (q[:, lo:lo + NOPE_DIM] * Q_SCALE).astype(BF16)
                q_ref[toks, lo + NOPE_DIM:lo + HEAD_W] = (
                    q[:, lo + NOPE_DIM:lo + HEAD_W] * (keep * Q_SCALE)).astype(BF16)
        _store_kv(ckv.astype(BF16), krz.astype(BF16), wuk_ref, wuv_ref, k_ref, v_ref, toks, pipe_layout=rope)
        if not rope:
            ckv_ref[toks, :] = ckv
            kr_ref[toks, :] = krb[:, :ROPE_DIM]


def _cachekv_kernel(ckv_ref, krz_ref, wuk_ref, wuvt_ref, k_ref, vt_ref):
    _store_kv(ckv_ref[...].astype(BF16), krz_ref[...].astype(BF16), wuk_ref, wuvt_ref, k_ref, vt_ref,
              slice(None), pipe_layout=True)


def _attn_kernel(q_ref, k_ref, v_ref, o_ref, *, seq):
    for b in range(q_ref.shape[0] // seq):
        rows = slice(b * seq, (b + 1) * seq)
        for hh in range(N_HEADS):
            kcols = slice(hh * HEAD_W, (hh + 1) * HEAD_W)
            vcols = slice(hh * V_DIM, (hh + 1) * V_DIM)
            s = lax.dot_general(q_ref[rows, kcols], k_ref[rows, kcols], _NT, preferred_element_type=F32)
            p = jnp.exp2(s - jnp.max(s, axis=-1, keepdims=True))
            l = jnp.sum(p, axis=-1, keepdims=True)
            acc = jnp.dot(p.astype(BF16), v_ref[rows, vcols], preferred_element_type=F32)
            o_ref[rows, vcols] = (acc / l).astype(BF16)


def _fold8(x, op):
    return op(x.reshape(x.shape[0] // 8, 8, x.shape[1]), axis=0)


def _attn_pipe_kernel(qt_ref, kn_ref, vtn_ref, kc_ref, vtc_ref, o_ref, st0_ref, st1_ref, m0_ref, m1_ref):
    sources = ((kn_ref, vtn_ref), (kc_ref, vtc_ref))
    heads = qt_ref.shape[0] // HEAD_W
    units = [(hh, r) for hh in range(heads) for r in range(qt_ref.shape[1] // ATTN_SUB_ROWS)]
    bufs = ((st0_ref, m0_ref), (st1_ref, m1_ref))

    def chunks():
        off = 0
        for k_ref, vt_ref in sources:
            for c in range(k_ref.shape[1] // ATTN_KEY_CHUNK):
                keys = slice(c * ATTN_KEY_CHUNK, (c + 1) * ATTN_KEY_CHUNK)
                yield k_ref, vt_ref, keys, slice(off, off + ATTN_KEY_CHUNK)
                off += ATTN_KEY_CHUNK

    def sub_rows(r):
        return slice(r * ATTN_SUB_ROWS, (r + 1) * ATTN_SUB_ROWS)

    def score_phase(i):
        hh, r = units[i]
        st_ref, m_ref = bufs[i % 2]
        qt = qt_ref[hh * HEAD_W:(hh + 1) * HEAD_W, sub_rows(r)]
        m8 = None
        for k_ref, _, keys, srows in chunks():
            st = jnp.dot(k_ref[hh, keys, :], qt, preferred_element_type=F32)
            st_ref[srows, :] = st
            part = _fold8(st, jnp.max)
            m8 = part if m8 is None else jnp.maximum(m8, part)
        m_ref[...] = jnp.max(m8, axis=0, keepdims=True)

    def value_phase(i):
        hh, r = units[i]
        st_ref, m_ref = bufs[i % 2]
        vrows = slice(hh * V_DIM, (hh + 1) * V_DIM)
        m = m_ref[...]
        l8 = None
        acc = None
        for _, vt_ref, keys, srows in chunks():
            pt = jnp.exp2(st_ref[srows, :] - m)
            part = _fold8(pt, jnp.sum)
            pv = jnp.dot(vt_ref[vrows, keys], pt.astype(BF16), preferred_element_type=F32)
            l8 = part if l8 is None else l8 + part
            acc = pv if acc is None else acc + pv
        l = jnp.sum(l8, axis=0, keepdims=True)
        o_ref[sub_rows(r), vrows] = (acc / l).T.astype(BF16)

    score_phase(0)
    for i in range(1, len(units)):
        score_phase(i)
        value_phase(i - 1)
    value_phase(len(units) - 1)


class _Layer(NamedTuple):
    stacked: jax.Array
    layer: int


def _const_spec(const):
    if isinstance(const, _Layer):
        shape = const.stacked.shape[1:]
        index = (const.layer,) + (0,) * len(shape)
        return pl.BlockSpec((None,) + shape, lambda *_: index, pipeline_mode=pl.Buffered(1))
    zeros = (0,) * const.ndim
    return pl.BlockSpec(const.shape, lambda *_: zeros, pipeline_mode=pl.Buffered(1))


def _const_arg(const):
    return const.stacked if isinstance(const, _Layer) else const


def _token_params():
    return pltpu.CompilerParams(dimension_semantics=("arbitrary",), vmem_limit_bytes=VMEM_LIMIT_BYTES)


def _rotate_half_cols(w):
    ws = w.reshape(w.shape[:-1] + (2, 2, ROPE_AXIS_PAIRS))
    return jnp.concatenate([-ws[..., 1:, :], ws[..., :1, :]], axis=-2).reshape(w.shape)


def _axial_rope_tables(n_tokens):
    rows = n_tokens // GRID_W
    row = jnp.repeat(jnp.arange(rows), GRID_W)
    col = jnp.tile(jnp.arange(GRID_W), rows)
    inv = 1.0 / (ROPE_BASE ** (jnp.arange(ROPE_AXIS_PAIRS, dtype=F32) / ROPE_AXIS_PAIRS))
    ang = jnp.stack([row, col], axis=-1).astype(F32)[:, :, None] * inv
    ang = jnp.broadcast_to(ang[:, :, None, :], (rows * GRID_W, 2, 2, ROPE_AXIS_PAIRS))
    ang = ang.reshape(rows * GRID_W, ROPE_DIM)
    pad = ((0, 0), (0, ROPE_DIM))
    return jnp.pad(jnp.cos(ang), pad), jnp.pad(jnp.sin(ang), pad)


class _Stream:
    def __init__(self, x, first_mod_row, shared_mod):
        self.batch, self.seq, _ = x.shape
        self.x = x.reshape(self.batch * self.seq, D_MODEL)
        self.tokens = self.batch * self.seq
        self.tiles = self.tokens // TOKEN_TILE
        tiles_per_batch = self.seq // TOKEN_TILE
        if shared_mod:
            self.mod_row = lambda t: first_mod_row
        else:
            self.mod_row = lambda t: first_mod_row + t // tiles_per_batch
        self.tiles_per_batch = tiles_per_batch

    def tile_spec(self, width):
        return pl.BlockSpec((TOKEN_TILE, width), lambda t: (t, 0))

    def mod_spec(self, layer):
        return pl.BlockSpec((None, None, 1, 6 * D_MODEL), lambda t: (layer, self.mod_row(t), 0, 0))


def _gmlp_call(st, x, mod4, layer, n1, w_in, v_g, w_s, b_s, w_out):
    consts = (n1, w_in, v_g, w_s, b_s, w_out)
    return pl.pallas_call(
        _gmlp_kernel,
        grid=(st.tiles,),
        in_specs=[st.tile_spec(D_MODEL), st.mod_spec(layer)] + [_const_spec(a) for a in consts],
        out_specs=st.tile_spec(D_MODEL),
        out_shape=jax.ShapeDtypeStruct((st.tokens, D_MODEL), F32),
        scratch_shapes=[pltpu.VMEM((TOKEN_TILE, A_WIDTH), BF16)],
        compiler_params=_token_params(),
        name="gmlp",
    )(x, mod4, *map(_const_arg, consts))


def _ffn_call(st, x, mod4, layer, n2, w_in, w_out, attn=None, w_o=None, final_g=None):
    has_attn = attn is not None
    final = final_g is not None
    args = [x, mod4]
    specs = [st.tile_spec(D_MODEL), st.mod_spec(layer)]
    if has_attn:
        args += [attn, _const_arg(w_o)]
        specs += [st.tile_spec(N_HEADS * V_DIM), _const_spec(w_o)]
    consts = [n2, w_in, w_out] + ([final_g] if final else [])
    args += map(_const_arg, consts)
    specs += [_const_spec(a) for a in consts]
    return pl.pallas_call(
        functools.partial(_ffn_kernel, has_attn=has_attn, final=final),
        grid=(st.tiles,),
        in_specs=specs,
        out_specs=st.tile_spec(D_MODEL),
        out_shape=jax.ShapeDtypeStruct((st.tokens, D_MODEL), F32),
        compiler_params=_token_params(),
        name="ffn",
    )(*args)


def _transposed_tile_spec(features):
    return pl.BlockSpec((features, TOKEN_TILE), lambda t: (0, t))


def _head_major_tile_spec():
    return pl.BlockSpec((N_HEADS, TOKEN_TILE, HEAD_W), lambda t: (0, t, 0))


def _proj_call(st, x, mod4, layer, n1, w_dq, q_g, w_uq, w_dkv, kv_g, w_uk, w_uv, rope_tables=None):
    rope = rope_tables is not None
    consts = (n1, w_dq, q_g, w_uq, w_dkv, kv_g, w_uk, w_uv)
    args = [x, mod4, *map(_const_arg, consts)]
    specs = [st.tile_spec(D_MODEL), st.mod_spec(layer)] + [_const_spec(a) for a in consts]
    qw, vw = N_HEADS * HEAD_W, N_HEADS * V_DIM
    out_shape = [
        jax.ShapeDtypeStruct((qw, st.tokens) if rope else (st.tokens, qw), BF16),
        jax.ShapeDtypeStruct((N_HEADS, st.tokens, HEAD_W) if rope else (st.tokens, qw), BF16),
        jax.ShapeDtypeStruct((vw, st.tokens) if rope else (st.tokens, vw), BF16),
    ]
    out_specs = [_transposed_tile_spec(qw) if rope else st.tile_spec(qw),
                 _head_major_tile_spec() if rope else st.tile_spec(qw),
                 _transposed_tile_spec(vw) if rope else st.tile_spec(vw)]
    if rope:
        tpb = st.tiles_per_batch
        cos, sin = rope_tables
        table_spec = pl.BlockSpec((TOKEN_TILE, 2 * ROPE_DIM), lambda t: (t % tpb, 0))
        table_t_spec = pl.BlockSpec((2 * ROPE_DIM, TOKEN_TILE), lambda t: (0, t % tpb))
        args += [cos, sin, cos.T, sin.T]
        specs += [table_spec, table_spec, table_t_spec, table_t_spec]
    else:
        out_shape += [jax.ShapeDtypeStruct((st.tokens, KV_LORA), F32),
                      jax.ShapeDtypeStruct((st.tokens, ROPE_DIM), F32)]
        out_specs += [st.tile_spec(KV_LORA), st.tile_spec(ROPE_DIM)]
    return pl.pallas_call(
        functools.partial(_proj_kernel, rope=rope),
        grid=(st.tiles,),
        in_specs=specs,
        out_specs=out_specs,
        out_shape=out_shape,
        compiler_params=_token_params(),
        name="proj",
    )(*args)


def _cachekv_call(ckv, krz, w_uk, w_uvt):
    tokens = ckv.shape[0]
    tile = lambda width: pl.BlockSpec((TOKEN_TILE, width), lambda t: (t, 0))
    return pl.pallas_call(
        _cachekv_kernel,
        grid=(tokens // TOKEN_TILE,),
        in_specs=[tile(KV_LORA), tile(2 * ROPE_DIM), _const_spec(w_uk), _const_spec(w_uvt)],
        out_specs=[_head_major_tile_spec(), _transposed_tile_spec(N_HEADS * V_DIM)],
        out_shape=[jax.ShapeDtypeStruct((N_HEADS, tokens, HEAD_W), BF16),
                   jax.ShapeDtypeStruct((N_HEADS * V_DIM, tokens), BF16)],
        compiler_params=_token_params(),
        name="cachekv",
    )(ckv, krz, w_uk, w_uvt)


def _attn_call(batch, seq, q, k, v):
    rows = ATTN_BATCH_PER_STEP * seq
    spec = lambda width: pl.BlockSpec((rows, width), lambda b: (b, 0))
    return pl.pallas_call(
        functools.partial(_attn_kernel, seq=seq),
        grid=(batch // ATTN_BATCH_PER_STEP,),
        in_specs=[spec(N_HEADS * HEAD_W), spec(N_HEADS * HEAD_W), spec(N_HEADS * V_DIM)],
        out_specs=spec(N_HEADS * V_DIM),
        out_shape=jax.ShapeDtypeStruct((batch * seq, N_HEADS * V_DIM), BF16),
        compiler_params=_token_params(),
        name="attn",
    )(q, k, v)


def _attn_pipe_call(batch, seq, q, k_new, vt_new, k_cache, vt_cache):
    past = k_cache.shape[1] // batch
    kw, vw = ATTN_PIPE_HEADS * HEAD_W, ATTN_PIPE_HEADS * V_DIM
    return pl.pallas_call(
        _attn_pipe_kernel,
        grid=(batch, N_HEADS // ATTN_PIPE_HEADS),
        in_specs=[
            pl.BlockSpec((kw, seq), lambda b, h: (h, b)),
            pl.BlockSpec((ATTN_PIPE_HEADS, seq, HEAD_W), lambda b, h: (h, b, 0)),
            pl.BlockSpec((vw, seq), lambda b, h: (h, b)),
            pl.BlockSpec((ATTN_PIPE_HEADS, past, HEAD_W), lambda b, h: (h, b, 0)),
            pl.BlockSpec((vw, past), lambda b, h: (h, b)),
        ],
        out_specs=pl.BlockSpec((seq, vw), lambda b, h: (b, h)),
        out_shape=jax.ShapeDtypeStruct((batch * seq, N_HEADS * V_DIM), BF16),
        scratch_shapes=[pltpu.VMEM((seq + past, ATTN_SUB_ROWS), F32)] * 2
        + [pltpu.VMEM((1, ATTN_SUB_ROWS), F32)] * 2,
        compiler_params=pltpu.CompilerParams(
            dimension_semantics=("arbitrary", "arbitrary"), vmem_limit_bytes=VMEM_LIMIT_BYTES),
        name="attn_pipe",
    )(q, k_new, vt_new, k_cache, vt_cache)


def kernel(x_prompt, x_sample, cache_ckv, cache_krope, c, c_ctx, ada_w, ada_b, norm1_g, norm2_g, a_w_in, a_v_g, a_w_s, a_b_s, a_w_out, b_w_dq, b_q_g, b_w_uq, b_w_dkv, b_kv_g, b_w_ukv, b_w_o, f_w_in, f_w_out, final_g):
    dec_batch = x_sample.shape[0]
    row = lambda g: g.reshape(1, -1)

    cond = jnp.concatenate(
        [c_ctx[None, :], c, jnp.zeros((MOD_ROWS - 1 - dec_batch, D_MODEL), F32)], axis=0)
    mod = _mod_call(cond, ada_w, ada_b)
    mod4 = mod.reshape(mod.shape[0], MOD_ROWS, 1, 6 * D_MODEL)

    a_w_in_b = _Layer(a_w_in.astype(BF16), 0)
    a_w_s_b = _Layer(a_w_s.astype(BF16), 0)
    a_b_s_c = a_b_s[0][:, :, None]
    a_w_out_b = _Layer(a_w_out.astype(BF16), 0)
    f_w_in_b = f_w_in.astype(BF16)
    f_w_out_b = f_w_out.astype(BF16)
    w_dq = _Layer(b_w_dq.astype(BF16), 0)
    uq = b_w_uq[0].reshape(Q_LORA, N_HEADS, NOPE_DIM + ROPE_DIM)
    uq_rope = uq[..., NOPE_DIM:]
    w_uq = jnp.concatenate([uq[..., :NOPE_DIM], uq_rope, _rotate_half_cols(uq_rope)], axis=-1)
    w_uq = w_uq.reshape(Q_LORA, N_HEADS * HEAD_W).astype(BF16)
    dkv_rope = b_w_dkv[0][:, KV_LORA:]
    w_dkv = jnp.concatenate([b_w_dkv[0], _rotate_half_cols(dkv_rope)], axis=-1).astype(BF16)
    ukv = b_w_ukv[0].reshape(KV_LORA, N_HEADS, NOPE_DIM + V_DIM)
    w_uk = ukv[..., :NOPE_DIM].reshape(KV_LORA, N_HEADS * NOPE_DIM).astype(BF16)
    w_uv = ukv[..., NOPE_DIM:].reshape(KV_LORA, N_HEADS * V_DIM).astype(BF16)
    w_uvt = w_uv.T
    w_o = _Layer(b_w_o.astype(BF16), 0)

    outs = []
    for x_in, first_row, shared in ((x_prompt, 0, True), (x_sample, 1, False)):
        st = _Stream(x_in, first_row, shared)
        is_sample = not shared
        x = _gmlp_call(st, st.x, mod4, 0, row(norm1_g[0]), a_w_in_b, row(a_v_g[0]), a_w_s_b, a_b_s_c, a_w_out_b)
        x = _ffn_call(st, x, mod4, 0, row(norm2_g[0]), _Layer(f_w_in_b, 0), _Layer(f_w_out_b, 0))
        proj_head = (row(norm1_g[1]), w_dq, row(b_q_g[0]))
        proj_tail = (w_dkv, row(b_kv_g[0]), w_uk)
        if is_sample:
            qt, k_new, vt_new = _proj_call(st, x, mod4, 1, *proj_head, w_uq.T, *proj_tail, w_uvt,
                                           rope_tables=_axial_rope_tables(st.seq))
            past = cache_ckv.shape[2]
            krz = jnp.pad(cache_krope[:, 0].reshape(st.batch * past, ROPE_DIM), ((0, 0), (0, ROPE_DIM)))
            k_cache, vt_cache = _cachekv_call(
                cache_ckv[:, 0].reshape(st.batch * past, KV_LORA), krz, w_uk, w_uvt)
            attn = _attn_pipe_call(st.batch, st.seq, qt, k_new, vt_new, k_cache, vt_cache)
        else:
            q, k_new, v_new, ckv, kr = _proj_call(st, x, mod4, 1, *proj_head, w_uq, *proj_tail, w_uv)
            attn = _attn_call(st.batch, st.seq, q, k_new, v_new)
            outs += [ckv.reshape(st.batch, 1, st.seq, KV_LORA), kr.reshape(st.batch, 1, st.seq, ROPE_DIM)]
        y = _ffn_call(st, x, mod4, 1, row(norm2_g[1]), _Layer(f_w_in_b, 1), _Layer(f_w_out_b, 1),
                      attn=attn, w_o=w_o, final_g=row(final_g))
        outs.append(y.reshape(x_in.shape))

    new_ckv, new_krope, y_prompt, y_sample = outs
    return (y_prompt, y_sample, new_ckv, new_krope)
```

```python
import functools
import math
from typing import NamedTuple

import jax
import jax.numpy as jnp
from jax import lax
from jax.experimental import pallas as pl
from jax.experimental.pallas import tpu as pltpu

F32 = jnp.float32
BF16 = jnp.bfloat16

D_MODEL = 1024
EPS = 1e-6
CHUNK = 128
A_WIDTH = 2 * D_MODEL
A_GROUPS = 8
A_GROUP_DIM = A_WIDTH // A_GROUPS
N_HEADS = 8
Q_LORA = 512
KV_LORA = 256
NOPE_DIM = 128
ROPE_DIM = 64
V_DIM = 128
ROPE_AXIS_PAIRS = ROPE_DIM // 4
ROPE_BASE = 10000.0
GRID_W = 64
ATTN_SCALE = 1.0 / math.sqrt(NOPE_DIM + ROPE_DIM)
Q_SCALE = ATTN_SCALE * math.log2(math.e)
D_FF = ((8 * D_MODEL // 3 + 255) // 256) * 256
SQRT_HALF = math.sqrt(0.5)

HEAD_W = NOPE_DIM + 2 * ROPE_DIM
MOD_ROWS = 16
MOD_TN = 1536
TOKEN_TILE = 512
SUB_TILE = 256
ATTN_SUB_ROWS = 256
ATTN_KEY_CHUNK = 512
ATTN_PIPE_HEADS = 4
ATTN_BATCH_PER_STEP = 8
VMEM_LIMIT_BYTES = 56 * 1024 * 1024


def _rms(x, g):
    return x * lax.rsqrt(jnp.mean(x * x, axis=-1, keepdims=True) + EPS) * g


def _norm_modulate(x, g, mod, shift_chunk, scale_chunk):
    gain = g * (1.0 + _mod_chunk(mod, scale_chunk))
    inv = lax.rsqrt(jnp.mean(x * x, axis=-1, keepdims=True) + EPS)
    return x * inv * gain + _mod_chunk(mod, shift_chunk)


def _gelu(z):
    return 0.5 * z * (1.0 + lax.erf(z * SQRT_HALF))


def _mod_chunk(mod, k):
    return mod[:, k * D_MODEL:(k + 1) * D_MODEL]


def _mod_kernel(c_ref, w_ref, b_ref, o_ref):
    s = jax.nn.silu(c_ref[...]).astype(BF16)
    o_ref[...] = jnp.dot(s, w_ref[...].astype(BF16), preferred_element_type=F32) + b_ref[...]


def _mod_call(cond, ada_w, ada_b):
    depth = ada_w.shape[0]
    n = ada_w.shape[2]
    return pl.pallas_call(
        _mod_kernel,
        grid=(depth, n // MOD_TN),
        in_specs=[
            pl.BlockSpec((MOD_ROWS, D_MODEL), lambda i, j: (0, 0)),
            pl.BlockSpec((None, D_MODEL, MOD_TN), lambda i, j: (i, 0, j)),
            pl.BlockSpec((None, 1, MOD_TN), lambda i, j: (i, 0, j)),
        ],
        out_specs=pl.BlockSpec((None, MOD_ROWS, MOD_TN), lambda i, j: (i, 0, j)),
        out_shape=jax.ShapeDtypeStruct((depth, MOD_ROWS, n), F32),
        compiler_params=pltpu.CompilerParams(
            dimension_semantics=("arbitrary", "arbitrary"), vmem_limit_bytes=VMEM_LIMIT_BYTES),
        name="mod",
    )(cond, ada_w, ada_b.reshape(depth, 1, n))


def _gmlp_kernel(x_ref, mod_ref, n1_ref, win_ref, vg_ref, ws_ref, bs_ref, wout_ref, o_ref, y_ref):
    x = x_ref[...]
    mod = mod_ref[...]
    hb = _norm_modulate(x, n1_ref[...], mod, 0, 1).astype(BF16)
    v = jnp.dot(hb, win_ref[:, A_WIDTH:], preferred_element_type=F32)
    v = _rms(_gelu(v), vg_ref[...]).astype(BF16)
    u = _gelu(jnp.dot(hb, win_ref[:, :A_WIDTH], preferred_element_type=F32))
    for c in range(x.shape[0] // CHUNK):
        rows = slice(c * CHUNK, (c + 1) * CHUNK)
        for g in range(A_GROUPS):
            cols = slice(g * A_GROUP_DIM, (g + 1) * A_GROUP_DIM)
            s = jnp.dot(ws_ref[g], v[rows, cols], preferred_element_type=F32) + bs_ref[g]
            y_ref[rows, cols] = (u[rows, cols] * s).astype(BF16)
    out = jnp.dot(y_ref[...], wout_ref[...], preferred_element_type=F32)
    o_ref[...] = x + _mod_chunk(mod, 2) * out


def _ffn_kernel(*refs, has_attn, final):
    refs = list(refs)
    x_ref, mod_ref = refs[:2]
    pos = 2
    if has_attn:
        a_ref, wo_ref = refs[pos:pos + 2]
        pos += 2
    n2_ref, win_ref, wout_ref = refs[pos:pos + 3]
    pos += 3
    if final:
        fg_ref = refs[pos]
        pos += 1
    o_ref = refs[pos]

    mod = mod_ref[...]
    units = [slice(r, r + SUB_TILE) for r in range(0, x_ref.shape[0], SUB_TILE)]
    heads = []
    for rows in units:
        x = x_ref[rows, :]
        if has_attn:
            x = x + _mod_chunk(mod, 2) * jnp.dot(a_ref[rows, :], wo_ref[...], preferred_element_type=F32)
        heads.append((x, _norm_modulate(x, n2_ref[...], mod, 3, 4).astype(BF16)))
    for rows, (x, h) in zip(units, heads):
        gu = jnp.dot(h, win_ref[...], preferred_element_type=F32)
        a = (jax.nn.silu(gu[:, :D_FF]) * gu[:, D_FF:]).astype(BF16)
        y = x + _mod_chunk(mod, 5) * jnp.dot(a, wout_ref[...], preferred_element_type=F32)
        if final:
            y = _rms(y, fg_ref[...])
        o_ref[rows, :] = y


_NT = (((1,), (1,)), ((), ()))


def _store_kv(ckv_bf16, krz_bf16, wuk_ref, wuv_ref, k_ref, v_ref, toks, pipe_layout):
    kn = jnp.dot(ckv_bf16, wuk_ref[...], preferred_element_type=F32)
    for hh in range(N_HEADS):
        kn_h = kn[:, hh * NOPE_DIM:(hh + 1) * NOPE_DIM].astype(BF16)
        if pipe_layout:
            k_ref[hh, toks, :NOPE_DIM] = kn_h
            k_ref[hh, toks, NOPE_DIM:] = krz_bf16
        else:
            k_ref[toks, hh * HEAD_W:hh * HEAD_W + NOPE_DIM] = kn_h
            k_ref[toks, hh * HEAD_W + NOPE_DIM:(hh + 1) * HEAD_W] = krz_bf16
    if pipe_layout:
        v_ref[:, toks] = lax.dot_general(wuv_ref[...], ckv_bf16, _NT, preferred_element_type=F32).astype(BF16)
    else:
        v_ref[toks, :] = jnp.dot(ckv_bf16, wuv_ref[...], preferred_element_type=F32).astype(BF16)


def _proj_kernel(*refs, rope):
    refs = list(refs)
    x_ref, mod_ref, n1_ref, wdq_ref, qg_ref, wuq_ref, wdkv_ref, kvg_ref, wuk_ref, wuv_ref = refs[:10]
    pos = 10
    if rope:
        cos_ref, sin_ref, cost_ref, sint_ref = refs[pos:pos + 4]
        pos += 4
    q_ref, k_ref, v_ref = refs[pos:pos + 3]
    pos += 3
    if not rope:
        ckv_ref, kr_ref = refs[pos:pos + 2]

    mod = mod_ref[...]
    units = [slice(0, x_ref.shape[0])]
    heads = [_norm_modulate(x_ref[toks, :], n1_ref[...], mod, 0, 1).astype(BF16) for toks in units]

    for toks, h in zip(units, heads):
        ql = _rms(jnp.dot(h, wdq_ref[...], preferred_element_type=F32), qg_ref[...]).astype(BF16)
        kv = jnp.dot(h, wdkv_ref[...], preferred_element_type=F32)
        ckv = _rms(kv[:, :KV_LORA], kvg_ref[...])
        krb = kv[:, KV_LORA:]
        if rope:
            cs, sn = cos_ref[toks, :], sin_ref[toks, :]
            cst, snt = cost_ref[:, toks], sint_ref[:, toks]
            krz = krb * cs + pltpu.roll(krb, ROPE_DIM, 1) * sn
            qt = lax.dot_general(wuq_ref[...], ql, _NT, preferred_element_type=F32)
            for hh in range(N_HEADS):
                lo = hh * HEAD_W
                q_ref[lo:lo + NOPE_DIM, toks] = (qt[lo:lo + NOPE_DIM, :] * Q_SCALE).astype(BF16)
                b = qt[lo + NOPE_DIM:lo + HEAD_W, :]
                roped = b * cst + pltpu.roll(b, ROPE_DIM, 0) * snt
                q_ref[lo + NOPE_DIM:lo + HEAD_W, toks] = (roped * Q_SCALE).astype(BF16)
        else:
            keep = (lax.broadcasted_iota(jnp.int32, (1, 2 * ROPE_DIM), 1) < ROPE_DIM).astype(F32)
            krz = krb * keep
            q = jnp.dot(ql, wuq_ref[...], preferred_element_type=F32)
            for hh in range(N_HEADS):
                lo = hh * HEAD_W
                q_ref[toks, lo:lo + NOPE_DIM] = (q[:, lo:lo + NOPE_DIM] * Q_SCALE).astype(BF16)
                q_ref[toks, lo + NOPE_DIM:lo + HEAD_W] = (
                    q[:, lo + NOPE_DIM:lo + HEAD_W] * (keep * Q_SCALE)).astype(BF16)
        _store_kv(ckv.astype(BF16), krz.astype(BF16), wuk_ref, wuv_ref, k_ref, v_ref, toks, pipe_layout=rope)
        if not rope:
            ckv_ref[toks, :] = ckv
            kr_ref[toks, :] = krb[:, :ROPE_DIM]


def _cachekv_kernel(ckv_ref, krz_ref, wuk_ref, wuvt_ref, k_ref, vt_ref):
    _store_kv(ckv_ref[...].astype(BF16), krz_ref[...].astype(BF16), wuk_ref, wuvt_ref, k_ref, vt_ref,
              slice(None), pipe_layout=True)


def _attn_kernel(q_ref, k_ref, v_ref, o_ref, *, seq):
    for b in range(q_ref.shape[0] // seq):
        rows = slice(b * seq, (b + 1) * seq)
        for hh in range(N_HEADS):
            kcols = slice(hh * HEAD_W, (hh + 1) * HEAD_W)
            vcols = slice(hh * V_DIM, (hh + 1) * V_DIM)
            s = lax.dot_general(q_ref[rows, kcols], k_ref[rows, kcols], _NT, preferred_element_type=F32)
            p = jnp.exp2(s - jnp.max(s, axis=-1, keepdims=True))
            l = jnp.sum(p, axis=-1, keepdims=True)
            acc = jnp.dot(p.astype(BF16), v_ref[rows, vcols], preferred_element_type=F32)
            o_ref[rows, vcols] = (acc / l).astype(BF16)


def _fold8(x, op):
    return op(x.reshape(x.shape[0] // 8, 8, x.shape[1]), axis=0)


def _attn_pipe_kernel(qt_ref, kn_ref, vtn_ref, kc_ref, vtc_ref, o_ref, st0_ref, st1_ref, m0_ref, m1_ref):
    sources = ((kn_ref, vtn_ref), (kc_ref, vtc_ref))
    heads = qt_ref.shape[0] // HEAD_W
    units = [(hh, r) for hh in range(heads) for r in range(qt_ref.shape[1] // ATTN_SUB_ROWS)]
    bufs = ((st0_ref, m0_ref), (st1_ref, m1_ref))

    def chunks():
        off = 0
        for k_ref, vt_ref in sources:
            for c in range(k_ref.shape[1] // ATTN_KEY_CHUNK):
                keys = slice(c * ATTN_KEY_CHUNK, (c + 1) * ATTN_KEY_CHUNK)
                yield k_ref, vt_ref, keys, slice(off, off + ATTN_KEY_CHUNK)
                off += ATTN_KEY_CHUNK

    def sub_rows(r):
        return slice(r * ATTN_SUB_ROWS, (r + 1) * ATTN_SUB_ROWS)

    def score_phase(i):
        hh, r = units[i]
        st_ref, m_ref = bufs[i % 2]
        qt = qt_ref[hh * HEAD_W:(hh + 1) * HEAD_W, sub_rows(r)]
        m8 = None
        for k_ref, _, keys, srows in chunks():
            st = jnp.dot(k_ref[hh, keys, :], qt, preferred_element_type=F32)
            st_ref[srows, :] = st
            part = _fold8(st, jnp.max)
            m8 = part if m8 is None else jnp.maximum(m8, part)
        m_ref[...] = jnp.max(m8, axis=0, keepdims=True)

    def value_phase(i):
        hh, r = units[i]
        st_ref, m_ref = bufs[i % 2]
        vrows = slice(hh * V_DIM, (hh + 1) * V_DIM)
        m = m_ref[...]
        l8 = None
        acc = None
        for _, vt_ref, keys, srows in chunks():
            pt = jnp.exp2(st_ref[srows, :] - m)
            part = _fold8(pt, jnp.sum)
            pv = jnp.dot(vt_ref[vrows, keys], pt.astype(BF16), preferred_element_type=F32)
            l8 = part if l8 is None else l8 + part
            acc = pv if acc is None else acc + pv
        l = jnp.sum(l8, axis=0, keepdims=True)
        o_ref[sub_rows(r), vrows] = (acc / l).T.astype(BF16)

    score_phase(0)
    for i in range(1, len(units)):
        score_phase(i)
        value_phase(i - 1)
    value_phase(len(units) - 1)


class _Layer(NamedTuple):
    stacked: jax.Array
    layer: int


def _const_spec(const):
    if isinstance(const, _Layer):
        shape = const.stacked.shape[1:]
        index = (const.layer,) + (0,) * len(shape)
        return pl.BlockSpec((None,) + shape, lambda *_: index, pipeline_mode=pl.Buffered(1))
    zeros = (0,) * const.ndim
    return pl.BlockSpec(const.shape, lambda *_: zeros, pipeline_mode=pl.Buffered(1))


def _const_arg(const):
    return const.stacked if isinstance(const, _Layer) else const


def _token_params():
    return pltpu.CompilerParams(dimension_semantics=("arbitrary",), vmem_limit_bytes=VMEM_LIMIT_BYTES)


def _rotate_half_cols(w):
    ws = w.reshape(w.shape[:-1] + (2, 2, ROPE_AXIS_PAIRS))
    return jnp.concatenate([-ws[..., 1:, :], ws[..., :1, :]], axis=-2).reshape(w.shape)


def _axial_rope_tables(n_tokens):
    rows = n_tokens // GRID_W
    row = jnp.repeat(jnp.arange(rows), GRID_W)
    col = jnp.tile(jnp.arange(GRID_W), rows)
    inv = 1.0 / (ROPE_BASE ** (jnp.arange(ROPE_AXIS_PAIRS, dtype=F32) / ROPE_AXIS_PAIRS))
    ang = jnp.stack([row, col], axis=-1).astype(F32)[:, :, None] * inv
    ang = jnp.broadcast_to(ang[:, :, None, :], (rows * GRID_W, 2, 2, ROPE_AXIS_PAIRS))
    ang = ang.reshape(rows * GRID_W, ROPE_DIM)
    pad = ((0, 0), (0, ROPE_DIM))
    return jnp.pad(jnp.cos(ang), pad), jnp.pad(jnp.sin(ang), pad)


class _Stream:
    def __init__(self, x, first_mod_row, shared_mod):
        self.batch, self.seq, _ = x.shape
        self.x = x.reshape(self.batch * self.seq, D_MODEL)
        self.tokens = self.batch * self.seq
        self.tiles = self.tokens // TOKEN_TILE
        tiles_per_batch = self.seq // TOKEN_TILE
        if shared_mod:
            self.mod_row = lambda t: first_mod_row
        else:
            self.mod_row = lambda t: first_mod_row + t // tiles_per_batch
        self.tiles_per_batch = tiles_per_batch

    def tile_spec(self, width):
        return pl.BlockSpec((TOKEN_TILE, width), lambda t: (t, 0))

    def mod_spec(self, layer):
        return pl.BlockSpec((None, None, 1, 6 * D_MODEL), lambda t: (layer, self.mod_row(t), 0, 0))


def _gmlp_call(st, x, mod4, layer, n1, w_in, v_g, w_s, b_s, w_out):
    consts = (n1, w_in, v_g, w_s, b_s, w_out)
    return pl.pallas_call(
        _gmlp_kernel,
        grid=(st.tiles,),
        in_specs=[st.tile_spec(D_MODEL), st.mod_spec(layer)] + [_const_spec(a) for a in consts],
        out_specs=st.tile_spec(D_MODEL),
        out_shape=jax.ShapeDtypeStruct((st.tokens, D_MODEL), F32),
        scratch_shapes=[pltpu.VMEM((TOKEN_TILE, A_WIDTH), BF16)],
        compiler_params=_token_params(),
        name="gmlp",
    )(x, mod4, *map(_const_arg, consts))


def _ffn_call(st, x, mod4, layer, n2, w_in, w_out, attn=None, w_o=None, final_g=None):
    has_attn = attn is not None
    final = final_g is not None
    args = [x, mod4]
    specs = [st.tile_spec(D_MODEL), st.mod_spec(layer)]
    if has_attn:
        args += [attn, _const_arg(w_o)]
        specs += [st.tile_spec(N_HEADS * V_DIM), _const_spec(w_o)]
    consts = [n2, w_in, w_out] + ([final_g] if final else [])
    args += map(_const_arg, consts)
    specs += [_const_spec(a) for a in consts]
    return pl.pallas_call(
        functools.partial(_ffn_kernel, has_attn=has_attn, final=final),
        grid=(st.tiles,),
        in_specs=specs,
        out_specs=st.tile_spec(D_MODEL),
        out_shape=jax.ShapeDtypeStruct((st.tokens, D_MODEL), F32),
        compiler_params=_token_params(),
        name="ffn",
    )(*args)


def _transposed_tile_spec(features):
    return pl.BlockSpec((features, TOKEN_TILE), lambda t: (0, t))


def _head_major_tile_spec():
    return pl.BlockSpec((N_HEADS, TOKEN_TILE, HEAD_W), lambda t: (0, t, 0))


def _proj_call(st, x, mod4, layer, n1, w_dq, q_g, w_uq, w_dkv, kv_g, w_uk, w_uv, rope_tables=None):
    rope = rope_tables is not None
    consts = (n1, w_dq, q_g, w_uq, w_dkv, kv_g, w_uk, w_uv)
    args = [x, mod4, *map(_const_arg, consts)]
    specs = [st.tile_spec(D_MODEL), st.mod_spec(layer)] + [_const_spec(a) for a in consts]
    qw, vw = N_HEADS * HEAD_W, N_HEADS * V_DIM
    out_shape = [
        jax.ShapeDtypeStruct((qw, st.tokens) if rope else (st.tokens, qw), BF16),
        jax.ShapeDtypeStruct((N_HEADS, st.tokens, HEAD_W) if rope else (st.tokens, qw), BF16),
        jax.ShapeDtypeStruct((vw, st.tokens) if rope else (st.tokens, vw), BF16),
    ]
    out_specs = [_transposed_tile_spec(qw) if rope else st.tile_spec(qw),
                 _head_major_tile_spec() if rope else st.tile_spec(qw),
                 _transposed_tile_spec(vw) if rope else st.tile_spec(vw)]
    if rope:
        tpb = st.tiles_per_batch
        cos, sin = rope_tables
        table_spec = pl.BlockSpec((TOKEN_TILE, 2 * ROPE_DIM), lambda t: (t % tpb, 0))
        table_t_spec = pl.BlockSpec((2 * ROPE_DIM, TOKEN_TILE), lambda t: (0, t % tpb))
        args += [cos, sin, cos.T, sin.T]
        specs += [table_spec, table_spec, table_t_spec, table_t_spec]
    else:
        out_shape += [jax.ShapeDtypeStruct((st.tokens, KV_LORA), F32),
                      jax.ShapeDtypeStruct((st.tokens, ROPE_DIM), F32)]
        out_specs += [st.tile_spec(KV_LORA), st.tile_spec(ROPE_DIM)]
    return pl.pallas_call(
        functools.partial(_proj_kernel, rope=rope),
        grid=(st.tiles,),
        in_specs=specs,
        out_specs=out_specs,
        out_shape=out_shape,
        compiler_params=_token_params(),
        name="proj",
    )(*args)


def _cachekv_call(ckv, krz, w_uk, w_uvt):
    tokens = ckv.shape[0]
    tile = lambda width: pl.BlockSpec((TOKEN_TILE, width), lambda t: (t, 0))
    return pl.pallas_call(
        _cachekv_kernel,
        grid=(tokens // TOKEN_TILE,),
        in_specs=[tile(KV_LORA), tile(2 * ROPE_DIM), _const_spec(w_uk), _const_spec(w_uvt)],
        out_specs=[_head_major_tile_spec(), _transposed_tile_spec(N_HEADS * V_DIM)],
        out_shape=[jax.ShapeDtypeStruct((N_HEADS, tokens, HEAD_W), BF16),
                   jax.ShapeDtypeStruct((N_HEADS * V_DIM, tokens), BF16)],
        compiler_params=_token_params(),
        name="cachekv",
    )(ckv, krz, w_uk, w_uvt)


def _attn_call(batch, seq, q, k, v):
    rows = ATTN_BATCH_PER_STEP * seq
    spec = lambda width: pl.BlockSpec((rows, width), lambda b: (b, 0))
    return pl.pallas_call(
        functools.partial(_attn_kernel, seq=seq),
        grid=(batch // ATTN_BATCH_PER_STEP,),
        in_specs=[spec(N_HEADS * HEAD_W), spec(N_HEADS * HEAD_W), spec(N_HEADS * V_DIM)],
        out_specs=spec(N_HEADS * V_DIM),
        out_shape=jax.ShapeDtypeStruct((batch * seq, N_HEADS * V_DIM), BF16),
        compiler_params=_token_params(),
        name="attn",
    )(q, k, v)


def _attn_pipe_call(batch, seq, q, k_new, vt_new, k_cache, vt_cache):
    past = k_cache.shape[1] // batch
    kw, vw = ATTN_PIPE_HEADS * HEAD_W, ATTN_PIPE_HEADS * V_DIM
    return pl.pallas_call(
        _attn_pipe_kernel,
        grid=(batch, N_HEADS // ATTN_PIPE_HEADS),
        in_specs=[
            pl.BlockSpec((kw, seq), lambda b, h: (h, b)),
            pl.BlockSpec((ATTN_PIPE_HEADS, seq, HEAD_W), lambda b, h: (h, b, 0)),
            pl.BlockSpec((vw, seq), lambda b, h: (h, b)),
            pl.BlockSpec((ATTN_PIPE_HEADS, past, HEAD_W), lambda b, h: (h, b, 0)),
            pl.BlockSpec((vw, past), lambda b, h: (h, b)),
        ],
        out_specs=pl.BlockSpec((seq, vw), lambda b, h: (b, h)),
        out_shape=jax.ShapeDtypeStruct((batch * seq, N_HEADS * V_DIM), BF16),
        scratch_shapes=[pltpu.VMEM((seq + past, ATTN_SUB_ROWS), F32)] * 2
        + [pltpu.VMEM((1, ATTN_SUB_ROWS), F32)] * 2,
        compiler_params=pltpu.CompilerParams(
            dimension_semantics=("arbitrary", "arbitrary"), vmem_limit_bytes=VMEM_LIMIT_BYTES),
        name="attn_pipe",
    )(q, k_new, vt_new, k_cache, vt_cache)


def kernel(x_prompt, x_sample, cache_ckv, cache_krope, c, c_ctx, ada_w, ada_b, norm1_g, norm2_g, a_w_in, a_v_g, a_w_s, a_b_s, a_w_out, b_w_dq, b_q_g, b_w_uq, b_w_dkv, b_kv_g, b_w_ukv, b_w_o, f_w_in, f_w_out, final_g):
    dec_batch = x_sample.shape[0]
    row = lambda g: g.reshape(1, -1)

    cond = jnp.concatenate(
        [c_ctx[None, :], c, jnp.zeros((MOD_ROWS - 1 - dec_batch, D_MODEL), F32)], axis=0)
    mod = _mod_call(cond, ada_w, ada_b)
    mod4 = mod.reshape(mod.shape[0], MOD_ROWS, 1, 6 * D_MODEL)

    a_w_in_b = _Layer(a_w_in.astype(BF16), 0)
    a_w_s_b = _Layer(a_w_s.astype(BF16), 0)
    a_b_s_c = a_b_s[0][:, :, None]
    a_w_out_b = _Layer(a_w_out.astype(BF16), 0)
    f_w_in_b = f_w_in.astype(BF16)
    f_w_out_b = f_w_out.astype(BF16)
    w_dq = _Layer(b_w_dq.astype(BF16), 0)
    uq = b_w_uq[0].reshape(Q_LORA, N_HEADS, NOPE_DIM + ROPE_DIM)
    uq_rope = uq[..., NOPE_DIM:]
    w_uq = jnp.concatenate([uq[..., :NOPE_DIM], uq_rope, _rotate_half_cols(uq_rope)], axis=-1)
    w_uq = w_uq.reshape(Q_LORA, N_HEADS * HEAD_W).astype(BF16)
    dkv_rope = b_w_dkv[0][:, KV_LORA:]
    w_dkv = jnp.concatenate([b_w_dkv[0], _rotate_half_cols(dkv_rope)], axis=-1).astype(BF16)
    ukv = b_w_ukv[0].reshape(KV_LORA, N_HEADS, NOPE_DIM + V_DIM)
    w_uk = ukv[..., :NOPE_DIM].reshape(KV_LORA, N_HEADS * NOPE_DIM).astype(BF16)
    w_uv = ukv[..., NOPE_DIM:].reshape(KV_LORA, N_HEADS * V_DIM).astype(BF16)
    w_uvt = w_uv.T
    w_o = _Layer(b_w_o.astype(BF16), 0)

    outs = []
    for x_in, first_row, shared in ((x_prompt, 0, True), (x_sample, 1, False)):
        st = _Stream(x_in, first_row, shared)
        is_sample = not shared
        x = _gmlp_call(st, st.x, mod4, 0, row(norm1_g[0]), a_w_in_b, row(a_v_g[0]), a_w_s_b, a_b_s_c, a_w_out_b)
        x = _ffn_call(st, x, mod4, 0, row(norm2_g[0]), _Layer(f_w_in_b, 0), _Layer(f_w_out_b, 0))
        proj_head = (row(norm1_g[1]), w_dq, row(b_q_g[0]))
        proj_tail = (w_dkv, row(b_kv_g[0]), w_uk)
        if is_sample:
            qt, k_new, vt_new = _proj_call(st, x, mod4, 1, *proj_head, w_uq.T, *proj_tail, w_uvt,
                                           rope_tables=_axial_rope_tables(st.seq))
            past = cache_ckv.shape[2]
            krz = jnp.pad(cache_krope[:, 0].reshape(st.batch * past, ROPE_DIM), ((0, 0), (0, ROPE_DIM)))
            k_cache, vt_cache = _cachekv_call(
                cache_ckv[:, 0].reshape(st.batch * past, KV_LORA), krz, w_uk, w_uvt)
            attn = _attn_pipe_call(st.batch, st.seq, qt, k_new, vt_new, k_cache, vt_cache)
        else:
            q, k_new, v_new, ckv, kr = _proj_call(st, x, mod4, 1, *proj_head, w_uq, *proj_tail, w_uv)
            attn = _attn_call(st.batch, st.seq, q, k_new, v_new)
            outs += [ckv.reshape(st.batch, 1, st.seq, KV_LORA), kr.reshape(st.batch, 1, st.seq, ROPE_DIM)]
        y = _ffn_call(st, x, mod4, 1, row(norm2_g[1]), _Layer(f_w_in_b, 1), _Layer(f_w_out_b, 1),
                      attn=attn, w_o=w_o, final_g=row(final_g))
        outs.append(y.reshape(x_in.shape))

    new_ckv, new_krope, y_prompt, y_sample = outs
    return (y_prompt, y_sample, new_ckv, new_krope)
```

```python
import functools
import math
from typing import NamedTuple

import jax
import jax.numpy as jnp
from jax import lax
from jax.experimental import pallas as pl
from jax.experimental.pallas import tpu as pltpu

F32 = jnp.float32
BF16 = jnp.bfloat16

D_MODEL = 1024
EPS = 1e-6
CHUNK = 128
A_WIDTH = 2 * D_MODEL
A_GROUPS = 8
A_GROUP_DIM = A_WIDTH // A_GROUPS
N_HEADS = 8
Q_LORA = 512
KV_LORA = 256
NOPE_DIM = 128
ROPE_DIM = 64
V_DIM = 128
ROPE_AXIS_PAIRS = ROPE_DIM // 4
ROPE_BASE = 10000.0
GRID_W = 64
ATTN_SCALE = 1.0 / math.sqrt(NOPE_DIM + ROPE_DIM)
Q_SCALE = ATTN_SCALE * math.log2(math.e)
D_FF = ((8 * D_MODEL // 3 + 255) // 256) * 256
SQRT_HALF = math.sqrt(0.5)

HEAD_W = NOPE_DIM + 2 * ROPE_DIM
MOD_ROWS = 16
MOD_TN = 1536
TOKEN_TILE = 512
SUB_TILE = 256
ATTN_SUB_ROWS = 256
ATTN_KEY_CHUNK = 512
ATTN_PIPE_HEADS = 4
ATTN_BATCH_PER_STEP = 8
VMEM_LIMIT_BYTES = 56 * 1024 * 1024


def _rms(x, g):
    return x * lax.rsqrt(jnp.mean(x * x, axis=-1, keepdims=True) + EPS) * g


def _norm_modulate(x, g, mod, shift_chunk, scale_chunk):
    gain = g * (1.0 + _mod_chunk(mod, scale_chunk))
    inv = lax.rsqrt(jnp.mean(x * x, axis=-1, keepdims=True) + EPS)
    return x * inv * gain + _mod_chunk(mod, shift_chunk)


def _gelu(z):
    return 0.5 * z * (1.0 + lax.erf(z * SQRT_HALF))


def _mod_chunk(mod, k):
    return mod[:, k * D_MODEL:(k + 1) * D_MODEL]


def _mod_kernel(c_ref, w_ref, b_ref, o_ref):
    s = jax.nn.silu(c_ref[...]).astype(BF16)
    o_ref[...] = jnp.dot(s, w_ref[...].astype(BF16), preferred_element_type=F32) + b_ref[...]


def _mod_call(cond, ada_w, ada_b):
    depth = ada_w.shape[0]
    n = ada_w.shape[2]
    return pl.pallas_call(
        _mod_kernel,
        grid=(depth, n // MOD_TN),
        in_specs=[
            pl.BlockSpec((MOD_ROWS, D_MODEL), lambda i, j: (0, 0)),
            pl.BlockSpec((None, D_MODEL, MOD_TN), lambda i, j: (i, 0, j)),
            pl.BlockSpec((None, 1, MOD_TN), lambda i, j: (i, 0, j)),
        ],
        out_specs=pl.BlockSpec((None, MOD_ROWS, MOD_TN), lambda i, j: (i, 0, j)),
        out_shape=jax.ShapeDtypeStruct((depth, MOD_ROWS, n), F32),
        compiler_params=pltpu.CompilerParams(
            dimension_semantics=("arbitrary", "arbitrary"), vmem_limit_bytes=VMEM_LIMIT_BYTES),
        name="mod",
    )(cond, ada_w, ada_b.reshape(depth, 1, n))


def _gmlp_kernel(x_ref, mod_ref, n1_ref, win_ref, vg_ref, ws_ref, bs_ref, wout_ref, o_ref, y_ref):
    x = x_ref[...]
    mod = mod_ref[...]
    hb = _norm_modulate(x, n1_ref[...], mod, 0, 1).astype(BF16)
    v = jnp.dot(hb, win_ref[:, A_WIDTH:], preferred_element_type=F32)
    v = _rms(_gelu(v), vg_ref[...]).astype(BF16)
    u = _gelu(jnp.dot(hb, win_ref[:, :A_WIDTH], preferred_element_type=F32))
    for c in range(x.shape[0] // CHUNK):
        rows = slice(c * CHUNK, (c + 1) * CHUNK)
        for g in range(A_GROUPS):
            cols = slice(g * A_GROUP_DIM, (g + 1) * A_GROUP_DIM)
            s = jnp.dot(ws_ref[g], v[rows, cols], preferred_element_type=F32) + bs_ref[g]
            y_ref[rows, cols] = (u[rows, cols] * s).astype(BF16)
    out = jnp.dot(y_ref[...], wout_ref[...], preferred_element_type=F32)
    o_ref[...] = x + _mod_chunk(mod, 2) * out


def _ffn_kernel(*refs, has_attn, final):
    refs = list(refs)
    x_ref, mod_ref = refs[:2]
    pos = 2
    if has_attn:
        a_ref, wo_ref = refs[pos:pos + 2]
        pos += 2
    n2_ref, win_ref, wout_ref = refs[pos:pos + 3]
    pos += 3
    if final:
        fg_ref = refs[pos]
        pos += 1
    o_ref = refs[pos]

    mod = mod_ref[...]
    units = [slice(r, r + SUB_TILE) for r in range(0, x_ref.shape[0], SUB_TILE)]
    heads = []
    for rows in units:
        x = x_ref[rows, :]
        if has_attn:
            x = x + _mod_chunk(mod, 2) * jnp.dot(a_ref[rows, :], wo_ref[...], preferred_element_type=F32)
        heads.append((x, _norm_modulate(x, n2_ref[...], mod, 3, 4).astype(BF16)))
    for rows, (x, h) in zip(units, heads):
        gu = jnp.dot(h, win_ref[...], preferred_element_type=F32)
        a = (jax.nn.silu(gu[:, :D_FF]) * gu[:, D_FF:]).astype(BF16)
        y = x + _mod_chunk(mod, 5) * jnp.dot(a, wout_ref[...], preferred_element_type=F32)
        if final:
            y = _rms(y, fg_ref[...])
        o_ref[rows, :] = y


_NT = (((1,), (1,)), ((), ()))


def _store_kv(ckv_bf16, krz_bf16, wuk_ref, wuv_ref, k_ref, v_ref, pipe_layout):
    kn = jnp.dot(ckv_bf16, wuk_ref[...], preferred_element_type=F32)
    for hh in range(N_HEADS):
        kn_h = kn[:, hh * NOPE_DIM:(hh + 1) * NOPE_DIM].astype(BF16)
        if pipe_layout:
            k_ref[hh, :, :NOPE_DIM] = kn_h
            k_ref[hh, :, NOPE_DIM:] = krz_bf16
        else:
            k_ref[:, hh * HEAD_W:hh * HEAD_W + NOPE_DIM] = kn_h
            k_ref[:, hh * HEAD_W + NOPE_DIM:(hh + 1) * HEAD_W] = krz_bf16
    if pipe_layout:
        v = lax.dot_general(wuv_ref[...], ckv_bf16, _NT, preferred_element_type=F32)
    else:
        v = jnp.dot(ckv_bf16, wuv_ref[...], preferred_element_type=F32)
    v_ref[...] = v.astype(BF16)


def _proj_kernel(*refs, rope):
    refs = list(refs)
    x_ref, mod_ref, n1_ref, wdq_ref, qg_ref, wuq_ref, wdkv_ref, kvg_ref, wuk_ref, wuv_ref = refs[:10]
    pos = 10
    if rope:
        cos_ref, sin_ref, cost_ref, sint_ref = refs[pos:pos + 4]
        pos += 4
    q_ref, k_ref, v_ref = refs[pos:pos + 3]
    pos += 3
    if not rope:
        ckv_ref, kr_ref = refs[pos:pos + 2]

    mod = mod_ref[...]
    h = _norm_modulate(x_ref[...], n1_ref[...], mod, 0, 1).astype(BF16)
    ql = _rms(jnp.dot(h, wdq_ref[...], preferred_element_type=F32), qg_ref[...]).astype(BF16)
    kv = jnp.dot(h, wdkv_ref[...], preferred_element_type=F32)
    ckv = _rms(kv[:, :KV_LORA], kvg_ref[...])
    krb = kv[:, KV_LORA:]

    if rope:
        cs, sn = cos_ref[...], sin_ref[...]
        cst, snt = cost_ref[...], sint_ref[...]
        krz = krb * cs + pltpu.roll(krb, ROPE_DIM, 1) * sn
        qt = lax.dot_general(wuq_ref[...], ql, _NT, preferred_element_type=F32)
        for hh in range(N_HEADS):
            lo = hh * HEAD_W
            q_ref[lo:lo + NOPE_DIM, :] = (qt[lo:lo + NOPE_DIM, :] * Q_SCALE).astype(BF16)
            b = qt[lo + NOPE_DIM:lo + HEAD_W, :]
            roped = b * cst + pltpu.roll(b, ROPE_DIM, 0) * snt
            q_ref[lo + NOPE_DIM:lo + HEAD_W, :] = (roped * Q_SCALE).astype(BF16)
    else:
        keep = (lax.broadcasted_iota(jnp.int32, (1, 2 * ROPE_DIM), 1) < ROPE_DIM).astype(F32)
        krz = krb * keep
        q = jnp.dot(ql, wuq_ref[...], preferred_element_type=F32)
        for hh in range(N_HEADS):
            lo = hh * HEAD_W
            q_ref[:, lo:lo + NOPE_DIM] = (q[:, lo:lo + NOPE_DIM] * Q_SCALE).astype(BF16)
            q_ref[:, lo + NOPE_DIM:lo + HEAD_W] = (q[:, lo + NOPE_DIM:lo + HEAD_W] * (keep * Q_SCALE)).astype(BF16)
    _store_kv(ckv.astype(BF16), krz.astype(BF16), wuk_ref, wuv_ref, k_ref, v_ref, pipe_layout=rope)
    if not rope:
        ckv_ref[...] = ckv
        kr_ref[...] = krb[:, :ROPE_DIM]


def _cachekv_kernel(ckv_ref, krz_ref, wuk_ref, wuvt_ref, k_ref, vt_ref):
    _store_kv(ckv_ref[...].astype(BF16), krz_ref[...].astype(BF16), wuk_ref, wuvt_ref, k_ref, vt_ref,
              pipe_layout=True)


def _attn_kernel(q_ref, k_ref, v_ref, o_ref, *, seq):
    for b in range(q_ref.shape[0] // seq):
        rows = slice(b * seq, (b + 1) * seq)
        for hh in range(N_HEADS):
            kcols = slice(hh * HEAD_W, (hh + 1) * HEAD_W)
            vcols = slice(hh * V_DIM, (hh + 1) * V_DIM)
            s = lax.dot_general(q_ref[rows, kcols], k_ref[rows, kcols], _NT, preferred_element_type=F32)
            p = jnp.exp2(s - jnp.max(s, axis=-1, keepdims=True))
            l = jnp.sum(p, axis=-1, keepdims=True)
            acc = jnp.dot(p.astype(BF16), v_ref[rows, vcols], preferred_element_type=F32)
            o_ref[rows, vcols] = (acc / l).astype(BF16)


def _fold8(x, op):
    return op(x.reshape(x.shape[0] // 8, 8, x.shape[1]), axis=0)


def _attn_pipe_kernel(qt_ref, kn_ref, vtn_ref, kc_ref, vtc_ref, o_ref, st0_ref, st1_ref, m0_ref, m1_ref):
    sources = ((kn_ref, vtn_ref), (kc_ref, vtc_ref))
    heads = qt_ref.shape[0] // HEAD_W
    units = [(hh, r) for hh in range(heads) for r in range(qt_ref.shape[1] // ATTN_SUB_ROWS)]
    bufs = ((st0_ref, m0_ref), (st1_ref, m1_ref))

    def chunks():
        off = 0
        for k_ref, vt_ref in sources:
            for c in range(k_ref.shape[1] // ATTN_KEY_CHUNK):
                keys = slice(c * ATTN_KEY_CHUNK, (c + 1) * ATTN_KEY_CHUNK)
                yield k_ref, vt_ref, keys, slice(off, off + ATTN_KEY_CHUNK)
                off += ATTN_KEY_CHUNK

    def sub_rows(r):
        return slice(r * ATTN_SUB_ROWS, (r + 1) * ATTN_SUB_ROWS)

    def score_phase(i):
        hh, r = units[i]
        st_ref, m_ref = bufs[i % 2]
        qt = qt_ref[hh * HEAD_W:(hh + 1) * HEAD_W, sub_rows(r)]
        m8 = None
        for k_ref, _, keys, srows in chunks():
            st = jnp.dot(k_ref[hh, keys, :], qt, preferred_element_type=F32)
            st_ref[srows, :] = st
            part = _fold8(st, jnp.max)
            m8 = part if m8 is None else jnp.maximum(m8, part)
        m_ref[...] = jnp.max(m8, axis=0, keepdims=True)

    def value_phase(i):
        hh, r = units[i]
        st_ref, m_ref = bufs[i % 2]
        vrows = slice(hh * V_DIM, (hh + 1) * V_DIM)
        m = m_ref[...]
        l8 = None
        acc = None
        for _, vt_ref, keys, srows in chunks():
            pt = jnp.exp2(st_ref[srows, :] - m)
            part = _fold8(pt, jnp.sum)
            pv = jnp.dot(vt_ref[vrows, keys], pt.astype(BF16), preferred_element_type=F32)
            l8 = part if l8 is None else l8 + part
            acc = pv if acc is None else acc + pv
        l = jnp.sum(l8, axis=0, keepdims=True)
        o_ref[sub_rows(r), vrows] = (acc / l).T.astype(BF16)

    score_phase(0)
    for i in range(1, len(units)):
        score_phase(i)
        value_phase(i - 1)
    value_phase(len(units) - 1)


class _Layer(NamedTuple):
    stacked: jax.Array
    layer: int


def _const_spec(const):
    if isinstance(const, _Layer):
        shape = const.stacked.shape[1:]
        index = (const.layer,) + (0,) * len(shape)
        return pl.BlockSpec((None,) + shape, lambda *_: index, pipeline_mode=pl.Buffered(1))
    zeros = (0,) * const.ndim
    return pl.BlockSpec(const.shape, lambda *_: zeros, pipeline_mode=pl.Buffered(1))


def _const_arg(const):
    return const.stacked if isinstance(const, _Layer) else const


def _token_params():
    return pltpu.CompilerParams(dimension_semantics=("arbitrary",), vmem_limit_bytes=VMEM_LIMIT_BYTES)


def _rotate_half_cols(w):
    ws = w.reshape(w.shape[:-1] + (2, 2, ROPE_AXIS_PAIRS))
    return jnp.concatenate([-ws[..., 1:, :], ws[..., :1, :]], axis=-2).reshape(w.shape)


def _axial_rope_tables(n_tokens):
    rows = n_tokens // GRID_W
    row = jnp.repeat(jnp.arange(rows), GRID_W)
    col = jnp.tile(jnp.arange(GRID_W), rows)
    inv = 1.0 / (ROPE_BASE ** (jnp.arange(ROPE_AXIS_PAIRS, dtype=F32) / ROPE_AXIS_PAIRS))
    ang = jnp.stack([row, col], axis=-1).astype(F32)[:, :, None] * inv
    ang = jnp.broadcast_to(ang[:, :, None, :], (rows * GRID_W, 2, 2, ROPE_AXIS_PAIRS))
    ang = ang.reshape(rows * GRID_W, ROPE_DIM)
    pad = ((0, 0), (0, ROPE_DIM))
    return jnp.pad(jnp.cos(ang), pad), jnp.pad(jnp.sin(ang), pad)


class _Stream:
    def __init__(self, x, first_mod_row, shared_mod):
        self.batch, self.seq, _ = x.shape
        self.x = x.reshape(self.batch * self.seq, D_MODEL)
        self.tokens = self.batch * self.seq
        self.tiles = self.tokens // TOKEN_TILE
        tiles_per_batch = self.seq // TOKEN_TILE
        if shared_mod:
            self.mod_row = lambda t: first_mod_row
        else:
            self.mod_row = lambda t: first_mod_row + t // tiles_per_batch
        self.tiles_per_batch = tiles_per_batch

    def tile_spec(self, width):
        return pl.BlockSpec((TOKEN_TILE, width), lambda t: (t, 0))

    def mod_spec(self, layer):
        return pl.BlockSpec((None, None, 1, 6 * D_MODEL), lambda t: (layer, self.mod_row(t), 0, 0))


def _gmlp_call(st, x, mod4, layer, n1, w_in, v_g, w_s, b_s, w_out):
    consts = (n1, w_in, v_g, w_s, b_s, w_out)
    return pl.pallas_call(
        _gmlp_kernel,
        grid=(st.tiles,),
        in_specs=[st.tile_spec(D_MODEL), st.mod_spec(layer)] + [_const_spec(a) for a in consts],
        out_specs=st.tile_spec(D_MODEL),
        out_shape=jax.ShapeDtypeStruct((st.tokens, D_MODEL), F32),
        scratch_shapes=[pltpu.VMEM((TOKEN_TILE, A_WIDTH), BF16)],
        compiler_params=_token_params(),
        name="gmlp",
    )(x, mod4, *map(_const_arg, consts))


def _ffn_call(st, x, mod4, layer, n2, w_in, w_out, attn=None, w_o=None, final_g=None):
    has_attn = attn is not None
    final = final_g is not None
    args = [x, mod4]
    specs = [st.tile_spec(D_MODEL), st.mod_spec(layer)]
    if has_attn:
        args += [attn, _const_arg(w_o)]
        specs += [st.tile_spec(N_HEADS * V_DIM), _const_spec(w_o)]
    consts = [n2, w_in, w_out] + ([final_g] if final else [])
    args += map(_const_arg, consts)
    specs += [_const_spec(a) for a in consts]
    return pl.pallas_call(
        functools.partial(_ffn_kernel, has_attn=has_attn, final=final),
        grid=(st.tiles,),
        in_specs=specs,
        out_specs=st.tile_spec(D_MODEL),
        out_shape=jax.ShapeDtypeStruct((st.tokens, D_MODEL), F32),
        compiler_params=_token_params(),
        name="ffn",
    )(*args)


def _transposed_tile_spec(features):
    return pl.BlockSpec((features, TOKEN_TILE), lambda t: (0, t))


def _head_major_tile_spec():
    return pl.BlockSpec((N_HEADS, TOKEN_TILE, HEAD_W), lambda t: (0, t, 0))


def _proj_call(st, x, mod4, layer, n1, w_dq, q_g, w_uq, w_dkv, kv_g, w_uk, w_uv, rope_tables=None):
    rope = rope_tables is not None
    consts = (n1, w_dq, q_g, w_uq, w_dkv, kv_g, w_uk, w_uv)
    args = [x, mod4, *map(_const_arg, consts)]
    specs = [st.tile_spec(D_MODEL), st.mod_spec(layer)] + [_const_spec(a) for a in consts]
    qw, vw = N_HEADS * HEAD_W, N_HEADS * V_DIM
    out_shape = [
        jax.ShapeDtypeStruct((qw, st.tokens) if rope else (st.tokens, qw), BF16),
        jax.ShapeDtypeStruct((N_HEADS, st.tokens, HEAD_W) if rope else (st.tokens, qw), BF16),
        jax.ShapeDtypeStruct((vw, st.tokens) if rope else (st.tokens, vw), BF16),
    ]
    out_specs = [_transposed_tile_spec(qw) if rope else st.tile_spec(qw),
                 _head_major_tile_spec() if rope else st.tile_spec(qw),
                 _transposed_tile_spec(vw) if rope else st.tile_spec(vw)]
    if rope:
        tpb = st.tiles_per_batch
        cos, sin = rope_tables
        table_spec = pl.BlockSpec((TOKEN_TILE, 2 * ROPE_DIM), lambda t: (t % tpb, 0))
        table_t_spec = pl.BlockSpec((2 * ROPE_DIM, TOKEN_TILE), lambda t: (0, t % tpb))
        args += [cos, sin, cos.T, sin.T]
        specs += [table_spec, table_spec, table_t_spec, table_t_spec]
    else:
        out_shape += [jax.ShapeDtypeStruct((st.tokens, KV_LORA), F32),
                      jax.ShapeDtypeStruct((st.tokens, ROPE_DIM), F32)]
        out_specs += [st.tile_spec(KV_LORA), st.tile_spec(ROPE_DIM)]
    return pl.pallas_call(
        functools.partial(_proj_kernel, rope=rope),
        grid=(st.tiles,),
        in_specs=specs,
        out_specs=out_specs,
        out_shape=out_shape,
        compiler_params=_token_params(),
        name="proj",
    )(*args)


def _cachekv_call(ckv, krz, w_uk, w_uvt):
    tokens = ckv.shape[0]
    tile = lambda width: pl.BlockSpec((TOKEN_TILE, width), lambda t: (t, 0))
    return pl.pallas_call(
        _cachekv_kernel,
        grid=(tokens // TOKEN_TILE,),
        in_specs=[tile(KV_LORA), tile(2 * ROPE_DIM), _const_spec(w_uk), _const_spec(w_uvt)],
        out_specs=[_head_major_tile_spec(), _transposed_tile_spec(N_HEADS * V_DIM)],
        out_shape=[jax.ShapeDtypeStruct((N_HEADS, tokens, HEAD_W), BF16),
                   jax.ShapeDtypeStruct((N_HEADS * V_DIM, tokens), BF16)],
        compiler_params=_token_params(),
        name="cachekv",
    )(ckv, krz, w_uk, w_uvt)


def _attn_call(batch, seq, q, k, v):
    rows = ATTN_BATCH_PER_STEP * seq
    spec = lambda width: pl.BlockSpec((rows, width), lambda b: (b, 0))
    return pl.pallas_call(
        functools.partial(_attn_kernel, seq=seq),
        grid=(batch // ATTN_BATCH_PER_STEP,),
        in_specs=[spec(N_HEADS * HEAD_W), spec(N_HEADS * HEAD_W), spec(N_HEADS * V_DIM)],
        out_specs=spec(N_HEADS * V_DIM),
        out_shape=jax.ShapeDtypeStruct((batch * seq, N_HEADS * V_DIM), BF16),
        compiler_params=_token_params(),
        name="attn",
    )(q, k, v)


def _attn_pipe_call(batch, seq, q, k_new, vt_new, k_cache, vt_cache):
    past = k_cache.shape[1] // batch
    kw, vw = ATTN_PIPE_HEADS * HEAD_W, ATTN_PIPE_HEADS * V_DIM
    return pl.pallas_call(
        _attn_pipe_kernel,
        grid=(batch, N_HEADS // ATTN_PIPE_HEADS),
        in_specs=[
            pl.BlockSpec((kw, seq), lambda b, h: (h, b)),
            pl.BlockSpec((ATTN_PIPE_HEADS, seq, HEAD_W), lambda b, h: (h, b, 0)),
            pl.BlockSpec((vw, seq), lambda b, h: (h, b)),
            pl.BlockSpec((ATTN_PIPE_HEADS, past, HEAD_W), lambda b, h: (h, b, 0)),
            pl.BlockSpec((vw, past), lambda b, h: (h, b)),
        ],
        out_specs=pl.BlockSpec((seq, vw), lambda b, h: (b, h)),
        out_shape=jax.ShapeDtypeStruct((batch * seq, N_HEADS * V_DIM), BF16),
        scratch_shapes=[pltpu.VMEM((seq + past, ATTN_SUB_ROWS), F32)] * 2
        + [pltpu.VMEM((1, ATTN_SUB_ROWS), F32)] * 2,
        compiler_params=pltpu.CompilerParams(
            dimension_semantics=("arbitrary", "arbitrary"), vmem_limit_bytes=VMEM_LIMIT_BYTES),
        name="attn_pipe",
    )(q, k_new, vt_new, k_cache, vt_cache)


def kernel(x_prompt, x_sample, cache_ckv, cache_krope, c, c_ctx, ada_w, ada_b, norm1_g, norm2_g, a_w_in, a_v_g, a_w_s, a_b_s, a_w_out, b_w_dq, b_q_g, b_w_uq, b_w_dkv, b_kv_g, b_w_ukv, b_w_o, f_w_in, f_w_out, final_g):
    dec_batch = x_sample.shape[0]
    row = lambda g: g.reshape(1, -1)

    cond = jnp.concatenate(
        [c_ctx[None, :], c, jnp.zeros((MOD_ROWS - 1 - dec_batch, D_MODEL), F32)], axis=0)
    mod = _mod_call(cond, ada_w, ada_b)
    mod4 = mod.reshape(mod.shape[0], MOD_ROWS, 1, 6 * D_MODEL)

    a_w_in_b = _Layer(a_w_in.astype(BF16), 0)
    a_w_s_b = _Layer(a_w_s.astype(BF16), 0)
    a_b_s_c = a_b_s[0][:, :, None]
    a_w_out_b = _Layer(a_w_out.astype(BF16), 0)
    f_w_in_b = f_w_in.astype(BF16)
    f_w_out_b = f_w_out.astype(BF16)
    w_dq = _Layer(b_w_dq.astype(BF16), 0)
    uq = b_w_uq[0].reshape(Q_LORA, N_HEADS, NOPE_DIM + ROPE_DIM)
    uq_rope = uq[..., NOPE_DIM:]
    w_uq = jnp.concatenate([uq[..., :NOPE_DIM], uq_rope, _rotate_half_cols(uq_rope)], axis=-1)
    w_uq = w_uq.reshape(Q_LORA, N_HEADS * HEAD_W).astype(BF16)
    dkv_rope = b_w_dkv[0][:, KV_LORA:]
    w_dkv = jnp.concatenate([b_w_dkv[0], _rotate_half_cols(dkv_rope)], axis=-1).astype(BF16)
    ukv = b_w_ukv[0].reshape(KV_LORA, N_HEADS, NOPE_DIM + V_DIM)
    w_uk = ukv[..., :NOPE_DIM].reshape(KV_LORA, N_HEADS * NOPE_DIM).astype(BF16)
    w_uv = ukv[..., NOPE_DIM:].reshape(KV_LORA, N_HEADS * V_DIM).astype(BF16)
    w_uvt = w_uv.T
    w_o = _Layer(b_w_o.astype(BF16), 0)

    outs = []
    for x_in, first_row, shared in ((x_prompt, 0, True), (x_sample, 1, False)):
        st = _Stream(x_in, first_row, shared)
        is_sample = not shared
        x = _gmlp_call(st, st.x, mod4, 0, row(norm1_g[0]), a_w_in_b, row(a_v_g[0]), a_w_s_b, a_b_s_c, a_w_out_b)
        x = _ffn_call(st, x, mod4, 0, row(norm2_g[0]), _Layer(f_w_in_b, 0), _Layer(f_w_out_b, 0))
        proj_head = (row(norm1_g[1]), w_dq, row(b_q_g[0]))
        proj_tail = (w_dkv, row(b_kv_g[0]), w_uk)
        if is_sample:
            qt, k_new, vt_new = _proj_call(st, x, mod4, 1, *proj_head, w_uq.T, *proj_tail, w_uvt,
                                           rope_tables=_axial_rope_tables(st.seq))
            past = cache_ckv.shape[2]
            krz = jnp.pad(cache_krope[:, 0].reshape(st.batch * past, ROPE_DIM), ((0, 0), (0, ROPE_DIM)))
            k_cache, vt_cache = _cachekv_call(
                cache_ckv[:, 0].reshape(st.batch * past, KV_LORA), krz, w_uk, w_uvt)
            attn = _attn_pipe_call(st.batch, st.seq, qt, k_new, vt_new, k_cache, vt_cache)
        else:
            q, k_new, v_new, ckv, kr = _proj_call(st, x, mod4, 1, *proj_head, w_uq, *proj_tail, w_uv)
            attn = _attn_call(st.batch, st.seq, q, k_new, v_new)
            outs += [ckv.reshape(st.batch, 1, st.seq, KV_LORA), kr.reshape(st.batch, 1, st.seq, ROPE_DIM)]
        y = _ffn_call(st, x, mod4, 1, row(norm2_g[1]), _Layer(f_w_in_b, 1), _Layer(f_w_out_b, 1),
                      attn=attn, w_o=w_o, final_g=row(final_g))
        outs.append(y.reshape(x_in.shape))

    new_ckv, new_krope, y_prompt, y_sample = outs
    return (y_prompt, y_sample, new_ckv, new_krope)
```

```python
import functools
import math
from typing import NamedTuple

import jax
import jax.numpy as jnp
from jax import lax
from jax.experimental import pallas as pl
from jax.experimental.pallas import tpu as pltpu

F32 = jnp.float32
BF16 = jnp.bfloat16

D_MODEL = 1024
EPS = 1e-6
CHUNK = 128
A_WIDTH = 2 * D_MODEL
A_GROUPS = 8
A_GROUP_DIM = A_WIDTH // A_GROUPS
N_HEADS = 8
Q_LORA = 512
KV_LORA = 256
NOPE_DIM = 128
ROPE_DIM = 64
V_DIM = 128
ROPE_AXIS_PAIRS = ROPE_DIM // 4
ROPE_BASE = 10000.0
GRID_W = 64
ATTN_SCALE = 1.0 / math.sqrt(NOPE_DIM + ROPE_DIM)
Q_SCALE = ATTN_SCALE * math.log2(math.e)
D_FF = ((8 * D_MODEL // 3 + 255) // 256) * 256
SQRT_HALF = math.sqrt(0.5)

HEAD_W = NOPE_DIM + 2 * ROPE_DIM
MOD_ROWS = 16
MOD_TN = 1536
TOKEN_TILE = 512
SUB_TILE = 128
ATTN_SUB_ROWS = 256
ATTN_KEY_CHUNK = 512
ATTN_PIPE_HEADS = 4
ATTN_BATCH_PER_STEP = 8
VMEM_LIMIT_BYTES = 56 * 1024 * 1024


def _rms(x, g):
    return x * lax.rsqrt(jnp.mean(x * x, axis=-1, keepdims=True) + EPS) * g


def _norm_modulate(x, g, mod, shift_chunk, scale_chunk):
    gain = g * (1.0 + _mod_chunk(mod, scale_chunk))
    inv = lax.rsqrt(jnp.mean(x * x, axis=-1, keepdims=True) + EPS)
    return x * inv * gain + _mod_chunk(mod, shift_chunk)


def _gelu(z):
    return 0.5 * z * (1.0 + lax.erf(z * SQRT_HALF))


def _mod_chunk(mod, k):
    return mod[:, k * D_MODEL:(k + 1) * D_MODEL]


def _mod_kernel(c_ref, w_ref, b_ref, o_ref):
    s = jax.nn.silu(c_ref[...]).astype(BF16)
    o_ref[...] = jnp.dot(s, w_ref[...].astype(BF16), preferred_element_type=F32) + b_ref[...]


def _mod_call(cond, ada_w, ada_b):
    depth = ada_w.shape[0]
    n = ada_w.shape[2]
    return pl.pallas_call(
        _mod_kernel,
        grid=(depth, n // MOD_TN),
        in_specs=[
            pl.BlockSpec((MOD_ROWS, D_MODEL), lambda i, j: (0, 0)),
            pl.BlockSpec((None, D_MODEL, MOD_TN), lambda i, j: (i, 0, j)),
            pl.BlockSpec((None, 1, MOD_TN), lambda i, j: (i, 0, j)),
        ],
        out_specs=pl.BlockSpec((None, MOD_ROWS, MOD_TN), lambda i, j: (i, 0, j)),
        out_shape=jax.ShapeDtypeStruct((depth, MOD_ROWS, n), F32),
        compiler_params=pltpu.CompilerParams(
            dimension_semantics=("arbitrary", "arbitrary"), vmem_limit_bytes=VMEM_LIMIT_BYTES),
        name="mod",
    )(cond, ada_w, ada_b.reshape(depth, 1, n))


def _gmlp_kernel(x_ref, mod_ref, n1_ref, win_ref, vg_ref, ws_ref, bs_ref, wout_ref, o_ref, y_ref):
    x = x_ref[...]
    mod = mod_ref[...]
    hb = _norm_modulate(x, n1_ref[...], mod, 0, 1).astype(BF16)
    v = jnp.dot(hb, win_ref[:, A_WIDTH:], preferred_element_type=F32)
    v = _rms(_gelu(v), vg_ref[...]).astype(BF16)
    u = _gelu(jnp.dot(hb, win_ref[:, :A_WIDTH], preferred_element_type=F32))
    for c in range(x.shape[0] // CHUNK):
        rows = slice(c * CHUNK, (c + 1) * CHUNK)
        for g in range(A_GROUPS):
            cols = slice(g * A_GROUP_DIM, (g + 1) * A_GROUP_DIM)
            s = jnp.dot(ws_ref[g], v[rows, cols], preferred_element_type=F32) + bs_ref[g]
            y_ref[rows, cols] = (u[rows, cols] * s).astype(BF16)
    out = jnp.dot(y_ref[...], wout_ref[...], preferred_element_type=F32)
    o_ref[...] = x + _mod_chunk(mod, 2) * out


def _ffn_kernel(*refs, has_attn, final):
    refs = list(refs)
    x_ref, mod_ref = refs[:2]
    pos = 2
    if has_attn:
        a_ref, wo_ref = refs[pos:pos + 2]
        pos += 2
    n2_ref, win_ref, wout_ref = refs[pos:pos + 3]
    pos += 3
    if final:
        fg_ref = refs[pos]
        pos += 1
    o_ref = refs[pos]

    mod = mod_ref[...]
    units = [slice(r, r + SUB_TILE) for r in range(0, x_ref.shape[0], SUB_TILE)]
    heads = []
    for rows in units:
        x = x_ref[rows, :]
        if has_attn:
            x = x + _mod_chunk(mod, 2) * jnp.dot(a_ref[rows, :], wo_ref[...], preferred_element_type=F32)
        heads.append((x, _norm_modulate(x, n2_ref[...], mod, 3, 4).astype(BF16)))
    for rows, (x, h) in zip(units, heads):
        gu = jnp.dot(h, win_ref[...], preferred_element_type=F32)
        a = (jax.nn.silu(gu[:, :D_FF]) * gu[:, D_FF:]).astype(BF16)
        y = x + _mod_chunk(mod, 5) * jnp.dot(a, wout_ref[...], preferred_element_type=F32)
        if final:
            y = _rms(y, fg_ref[...])
        o_ref[rows, :] = y


_NT = (((1,), (1,)), ((), ()))


def _store_kv(ckv_bf16, krz_bf16, wuk_ref, wuv_ref, k_ref, v_ref, pipe_layout):
    kn = jnp.dot(ckv_bf16, wuk_ref[...], preferred_element_type=F32)
    for hh in range(N_HEADS):
        kn_h = kn[:, hh * NOPE_DIM:(hh + 1) * NOPE_DIM].astype(BF16)
        if pipe_layout:
            k_ref[hh, :, :NOPE_DIM] = kn_h
            k_ref[hh, :, NOPE_DIM:] = krz_bf16
        else:
            k_ref[:, hh * HEAD_W:hh * HEAD_W + NOPE_DIM] = kn_h
            k_ref[:, hh * HEAD_W + NOPE_DIM:(hh + 1) * HEAD_W] = krz_bf16
    if pipe_layout:
        v = lax.dot_general(wuv_ref[...], ckv_bf16, _NT, preferred_element_type=F32)
    else:
        v = jnp.dot(ckv_bf16, wuv_ref[...], preferred_element_type=F32)
    v_ref[...] = v.astype(BF16)


def _proj_kernel(*refs, rope):
    refs = list(refs)
    x_ref, mod_ref, n1_ref, wdq_ref, qg_ref, wuq_ref, wdkv_ref, kvg_ref, wuk_ref, wuv_ref = refs[:10]
    pos = 10
    if rope:
        cos_ref, sin_ref, cost_ref, sint_ref = refs[pos:pos + 4]
        pos += 4
    q_ref, k_ref, v_ref = refs[pos:pos + 3]
    pos += 3
    if not rope:
        ckv_ref, kr_ref = refs[pos:pos + 2]

    mod = mod_ref[...]
    h = _norm_modulate(x_ref[...], n1_ref[...], mod, 0, 1).astype(BF16)
    ql = _rms(jnp.dot(h, wdq_ref[...], preferred_element_type=F32), qg_ref[...]).astype(BF16)
    kv = jnp.dot(h, wdkv_ref[...], preferred_element_type=F32)
    ckv = _rms(kv[:, :KV_LORA], kvg_ref[...])
    krb = kv[:, KV_LORA:]

    if rope:
        cs, sn = cos_ref[...], sin_ref[...]
        cst, snt = cost_ref[...], sint_ref[...]
        krz = krb * cs + pltpu.roll(krb, ROPE_DIM, 1) * sn
        qt = lax.dot_general(wuq_ref[...], ql, _NT, preferred_element_type=F32)
        for hh in range(N_HEADS):
            lo = hh * HEAD_W
            q_ref[lo:lo + NOPE_DIM, :] = (qt[lo:lo + NOPE_DIM, :] * Q_SCALE).astype(BF16)
            b = qt[lo + NOPE_DIM:lo + HEAD_W, :]
            roped = b * cst + pltpu.roll(b, ROPE_DIM, 0) * snt
            q_ref[lo + NOPE_DIM:lo + HEAD_W, :] = (roped * Q_SCALE).astype(BF16)
    else:
        keep = (lax.broadcasted_iota(jnp.int32, (1, 2 * ROPE_DIM), 1) < ROPE_DIM).astype(F32)
        krz = krb * keep
        q = jnp.dot(ql, wuq_ref[...], preferred_element_type=F32)
        for hh in range(N_HEADS):
            lo = hh * HEAD_W
            q_ref[:, lo:lo + NOPE_DIM] = (q[:, lo:lo + NOPE_DIM] * Q_SCALE).astype(BF16)
            q_ref[:, lo + NOPE_DIM:lo + HEAD_W] = (q[:, lo + NOPE_DIM:lo + HEAD_W] * (keep * Q_SCALE)).astype(BF16)
    _store_kv(ckv.astype(BF16), krz.astype(BF16), wuk_ref, wuv_ref, k_ref, v_ref, pipe_layout=rope)
    if not rope:
        ckv_ref[...] = ckv
        kr_ref[...] = krb[:, :ROPE_DIM]


def _cachekv_kernel(ckv_ref, krz_ref, wuk_ref, wuvt_ref, k_ref, vt_ref):
    _store_kv(ckv_ref[...].astype(BF16), krz_ref[...].astype(BF16), wuk_ref, wuvt_ref, k_ref, vt_ref,
              pipe_layout=True)


def _attn_kernel(q_ref, k_ref, v_ref, o_ref, *, seq):
    for b in range(q_ref.shape[0] // seq):
        rows = slice(b * seq, (b + 1) * seq)
        for hh in range(N_HEADS):
            kcols = slice(hh * HEAD_W, (hh + 1) * HEAD_W)
            vcols = slice(hh * V_DIM, (hh + 1) * V_DIM)
            s = lax.dot_general(q_ref[rows, kcols], k_ref[rows, kcols], _NT, preferred_element_type=F32)
            p = jnp.exp2(s - jnp.max(s, axis=-1, keepdims=True))
            l = jnp.sum(p, axis=-1, keepdims=True)
            acc = jnp.dot(p.astype(BF16), v_ref[rows, vcols], preferred_element_type=F32)
            o_ref[rows, vcols] = (acc / l).astype(BF16)


def _fold8(x, op):
    return op(x.reshape(x.shape[0] // 8, 8, x.shape[1]), axis=0)


def _attn_pipe_kernel(qt_ref, kn_ref, vtn_ref, kc_ref, vtc_ref, o_ref, st0_ref, st1_ref, m0_ref, m1_ref):
    sources = ((kn_ref, vtn_ref), (kc_ref, vtc_ref))
    heads = qt_ref.shape[0] // HEAD_W
    units = [(hh, r) for hh in range(heads) for r in range(qt_ref.shape[1] // ATTN_SUB_ROWS)]
    bufs = ((st0_ref, m0_ref), (st1_ref, m1_ref))

    def chunks():
        off = 0
        for k_ref, vt_ref in sources:
            for c in range(k_ref.shape[1] // ATTN_KEY_CHUNK):
                keys = slice(c * ATTN_KEY_CHUNK, (c + 1) * ATTN_KEY_CHUNK)
                yield k_ref, vt_ref, keys, slice(off, off + ATTN_KEY_CHUNK)
                off += ATTN_KEY_CHUNK

    def sub_rows(r):
        return slice(r * ATTN_SUB_ROWS, (r + 1) * ATTN_SUB_ROWS)

    def score_phase(i):
        hh, r = units[i]
        st_ref, m_ref = bufs[i % 2]
        qt = qt_ref[hh * HEAD_W:(hh + 1) * HEAD_W, sub_rows(r)]
        m8 = None
        for k_ref, _, keys, srows in chunks():
            st = jnp.dot(k_ref[hh, keys, :], qt, preferred_element_type=F32)
            st_ref[srows, :] = st
            part = _fold8(st, jnp.max)
            m8 = part if m8 is None else jnp.maximum(m8, part)
        m_ref[...] = jnp.max(m8, axis=0, keepdims=True)

    def value_phase(i):
        hh, r = units[i]
        st_ref, m_ref = bufs[i % 2]
        vrows = slice(hh * V_DIM, (hh + 1) * V_DIM)
        m = m_ref[...]
        l8 = None
        acc = None
        for _, vt_ref, keys, srows in chunks():
            pt = jnp.exp2(st_ref[srows, :] - m)
            part = _fold8(pt, jnp.sum)
            pv = jnp.dot(vt_ref[vrows, keys], pt.astype(BF16), preferred_element_type=F32)
            l8 = part if l8 is None else l8 + part
            acc = pv if acc is None else acc + pv
        l = jnp.sum(l8, axis=0, keepdims=True)
        o_ref[sub_rows(r), vrows] = (acc / l).T.astype(BF16)

    score_phase(0)
    for i in range(1, len(units)):
        score_phase(i)
        value_phase(i - 1)
    value_phase(len(units) - 1)


class _Layer(NamedTuple):
    stacked: jax.Array
    layer: int


def _const_spec(const):
    if isinstance(const, _Layer):
        shape = const.stacked.shape[1:]
        index = (const.layer,) + (0,) * len(shape)
        return pl.BlockSpec((None,) + shape, lambda *_: index, pipeline_mode=pl.Buffered(1))
    zeros = (0,) * const.ndim
    return pl.BlockSpec(const.shape, lambda *_: zeros, pipeline_mode=pl.Buffered(1))


def _const_arg(const):
    return const.stacked if isinstance(const, _Layer) else const


def _token_params():
    return pltpu.CompilerParams(dimension_semantics=("arbitrary",), vmem_limit_bytes=VMEM_LIMIT_BYTES)


def _rotate_half_cols(w):
    ws = w.reshape(w.shape[:-1] + (2, 2, ROPE_AXIS_PAIRS))
    return jnp.concatenate([-ws[..., 1:, :], ws[..., :1, :]], axis=-2).reshape(w.shape)


def _axial_rope_tables(n_tokens):
    rows = n_tokens // GRID_W
    row = jnp.repeat(jnp.arange(rows), GRID_W)
    col = jnp.tile(jnp.arange(GRID_W), rows)
    inv = 1.0 / (ROPE_BASE ** (jnp.arange(ROPE_AXIS_PAIRS, dtype=F32) / ROPE_AXIS_PAIRS))
    ang = jnp.stack([row, col], axis=-1).astype(F32)[:, :, None] * inv
    ang = jnp.broadcast_to(ang[:, :, None, :], (rows * GRID_W, 2, 2, ROPE_AXIS_PAIRS))
    ang = ang.reshape(rows * GRID_W, ROPE_DIM)
    pad = ((0, 0), (0, ROPE_DIM))
    return jnp.pad(jnp.cos(ang), pad), jnp.pad(jnp.sin(ang), pad)


class _Stream:
    def __init__(self, x, first_mod_row, shared_mod):
        self.batch, self.seq, _ = x.shape
        self.x = x.reshape(self.batch * self.seq, D_MODEL)
        self.tokens = self.batch * self.seq
        self.tiles = self.tokens // TOKEN_TILE
        tiles_per_batch = self.seq // TOKEN_TILE
        if shared_mod:
            self.mod_row = lambda t: first_mod_row
        else:
            self.mod_row = lambda t: first_mod_row + t // tiles_per_batch
        self.tiles_per_batch = tiles_per_batch

    def tile_spec(self, width):
        return pl.BlockSpec((TOKEN_TILE, width), lambda t: (t, 0))

    def mod_spec(self, layer):
        return pl.BlockSpec((None, None, 1, 6 * D_MODEL), lambda t: (layer, self.mod_row(t), 0, 0))


def _gmlp_call(st, x, mod4, layer, n1, w_in, v_g, w_s, b_s, w_out):
    consts = (n1, w_in, v_g, w_s, b_s, w_out)
    return pl.pallas_call(
        _gmlp_kernel,
        grid=(st.tiles,),
        in_specs=[st.tile_spec(D_MODEL), st.mod_spec(layer)] + [_const_spec(a) for a in consts],
        out_specs=st.tile_spec(D_MODEL),
        out_shape=jax.ShapeDtypeStruct((st.tokens, D_MODEL), F32),
        scratch_shapes=[pltpu.VMEM((TOKEN_TILE, A_WIDTH), BF16)],
        compiler_params=_token_params(),
        name="gmlp",
    )(x, mod4, *map(_const_arg, consts))


def _ffn_call(st, x, mod4, layer, n2, w_in, w_out, attn=None, w_o=None, final_g=None):
    has_attn = attn is not None
    final = final_g is not None
    args = [x, mod4]
    specs = [st.tile_spec(D_MODEL), st.mod_spec(layer)]
    if has_attn:
        args += [attn, _const_arg(w_o)]
        specs += [st.tile_spec(N_HEADS * V_DIM), _const_spec(w_o)]
    consts = [n2, w_in, w_out] + ([final_g] if final else [])
    args += map(_const_arg, consts)
    specs += [_const_spec(a) for a in consts]
    return pl.pallas_call(
        functools.partial(_ffn_kernel, has_attn=has_attn, final=final),
        grid=(st.tiles,),
        in_specs=specs,
        out_specs=st.tile_spec(D_MODEL),
        out_shape=jax.ShapeDtypeStruct((st.tokens, D_MODEL), F32),
        compiler_params=_token_params(),
        name="ffn",
    )(*args)


def _transposed_tile_spec(features):
    return pl.BlockSpec((features, TOKEN_TILE), lambda t: (0, t))


def _head_major_tile_spec():
    return pl.BlockSpec((N_HEADS, TOKEN_TILE, HEAD_W), lambda t: (0, t, 0))


def _proj_call(st, x, mod4, layer, n1, w_dq, q_g, w_uq, w_dkv, kv_g, w_uk, w_uv, rope_tables=None):
    rope = rope_tables is not None
    consts = (n1, w_dq, q_g, w_uq, w_dkv, kv_g, w_uk, w_uv)
    args = [x, mod4, *map(_const_arg, consts)]
    specs = [st.tile_spec(D_MODEL), st.mod_spec(layer)] + [_const_spec(a) for a in consts]
    qw, vw = N_HEADS * HEAD_W, N_HEADS * V_DIM
    out_shape = [
        jax.ShapeDtypeStruct((qw, st.tokens) if rope else (st.tokens, qw), BF16),
        jax.ShapeDtypeStruct((N_HEADS, st.tokens, HEAD_W) if rope else (st.tokens, qw), BF16),
        jax.ShapeDtypeStruct((vw, st.tokens) if rope else (st.tokens, vw), BF16),
    ]
    out_specs = [_transposed_tile_spec(qw) if rope else st.tile_spec(qw),
                 _head_major_tile_spec() if rope else st.tile_spec(qw),
                 _transposed_tile_spec(vw) if rope else st.tile_spec(vw)]
    if rope:
        tpb = st.tiles_per_batch
        cos, sin = rope_tables
        table_spec = pl.BlockSpec((TOKEN_TILE, 2 * ROPE_DIM), lambda t: (t % tpb, 0))
        table_t_spec = pl.BlockSpec((2 * ROPE_DIM, TOKEN_TILE), lambda t: (0, t % tpb))
        args += [cos, sin, cos.T, sin.T]
        specs += [table_spec, table_spec, table_t_spec, table_t_spec]
    else:
        out_shape += [jax.ShapeDtypeStruct((st.tokens, KV_LORA), F32),
                      jax.ShapeDtypeStruct((st.tokens, ROPE_DIM), F32)]
        out_specs += [st.tile_spec(KV_LORA), st.tile_spec(ROPE_DIM)]
    return pl.pallas_call(
        functools.partial(_proj_kernel, rope=rope),
        grid=(st.tiles,),
        in_specs=specs,
        out_specs=out_specs,
        out_shape=out_shape,
        compiler_params=_token_params(),
        name="proj",
    )(*args)


def _cachekv_call(ckv, krz, w_uk, w_uvt):
    tokens = ckv.shape[0]
    tile = lambda width: pl.BlockSpec((TOKEN_TILE, width), lambda t: (t, 0))
    return pl.pallas_call(
        _cachekv_kernel,
        grid=(tokens // TOKEN_TILE,),
        in_specs=[tile(KV_LORA), tile(2 * ROPE_DIM), _const_spec(w_uk), _const_spec(w_uvt)],
        out_specs=[_head_major_tile_spec(), _transposed_tile_spec(N_HEADS * V_DIM)],
        out_shape=[jax.ShapeDtypeStruct((N_HEADS, tokens, HEAD_W), BF16),
                   jax.ShapeDtypeStruct((N_HEADS * V_DIM, tokens), BF16)],
        compiler_params=_token_params(),
        name="cachekv",
    )(ckv, krz, w_uk, w_uvt)


def _attn_call(batch, seq, q, k, v):
    rows = ATTN_BATCH_PER_STEP * seq
    spec = lambda width: pl.BlockSpec((rows, width), lambda b: (b, 0))
    return pl.pallas_call(
        functools.partial(_attn_kernel, seq=seq),
        grid=(batch // ATTN_BATCH_PER_STEP,),
        in_specs=[spec(N_HEADS * HEAD_W), spec(N_HEADS * HEAD_W), spec(N_HEADS * V_DIM)],
        out_specs=spec(N_HEADS * V_DIM),
        out_shape=jax.ShapeDtypeStruct((batch * seq, N_HEADS * V_DIM), BF16),
        compiler_params=_token_params(),
        name="attn",
    )(q, k, v)


def _attn_pipe_call(batch, seq, q, k_new, vt_new, k_cache, vt_cache):
    past = k_cache.shape[1] // batch
    kw, vw = ATTN_PIPE_HEADS * HEAD_W, ATTN_PIPE_HEADS * V_DIM
    return pl.pallas_call(
        _attn_pipe_kernel,
        grid=(batch, N_HEADS // ATTN_PIPE_HEADS),
        in_specs=[
            pl.BlockSpec((kw, seq), lambda b, h: (h, b)),
            pl.BlockSpec((ATTN_PIPE_HEADS, seq, HEAD_W), lambda b, h: (h, b, 0)),
            pl.BlockSpec((vw, seq), lambda b, h: (h, b)),
            pl.BlockSpec((ATTN_PIPE_HEADS, past, HEAD_W), lambda b, h: (h, b, 0)),
            pl.BlockSpec((vw, past), lambda b, h: (h, b)),
        ],
        out_specs=pl.BlockSpec((seq, vw), lambda b, h: (b, h)),
        out_shape=jax.ShapeDtypeStruct((batch * seq, N_HEADS * V_DIM), BF16),
        scratch_shapes=[pltpu.VMEM((seq + past, ATTN_SUB_ROWS), F32)] * 2
        + [pltpu.VMEM((1, ATTN_SUB_ROWS), F32)] * 2,
        compiler_params=pltpu.CompilerParams(
            dimension_semantics=("arbitrary", "arbitrary"), vmem_limit_bytes=VMEM_LIMIT_BYTES),
        name="attn_pipe",
    )(q, k_new, vt_new, k_cache, vt_cache)


def kernel(x_prompt, x_sample, cache_ckv, cache_krope, c, c_ctx, ada_w, ada_b, norm1_g, norm2_g, a_w_in, a_v_g, a_w_s, a_b_s, a_w_out, b_w_dq, b_q_g, b_w_uq, b_w_dkv, b_kv_g, b_w_ukv, b_w_o, f_w_in, f_w_out, final_g):
    dec_batch = x_sample.shape[0]
    row = lambda g: g.reshape(1, -1)

    cond = jnp.concatenate(
        [c_ctx[None, :], c, jnp.zeros((MOD_ROWS - 1 - dec_batch, D_MODEL), F32)], axis=0)
    mod = _mod_call(cond, ada_w, ada_b)
    mod4 = mod.reshape(mod.shape[0], MOD_ROWS, 1, 6 * D_MODEL)

    a_w_in_b = _Layer(a_w_in.astype(BF16), 0)
    a_w_s_b = _Layer(a_w_s.astype(BF16), 0)
    a_b_s_c = a_b_s[0][:, :, None]
    a_w_out_b = _Layer(a_w_out.astype(BF16), 0)
    f_w_in_b = f_w_in.astype(BF16)
    f_w_out_b = f_w_out.astype(BF16)
    w_dq = _Layer(b_w_dq.astype(BF16), 0)
    uq = b_w_uq[0].reshape(Q_LORA, N_HEADS, NOPE_DIM + ROPE_DIM)
    uq_rope = uq[..., NOPE_DIM:]
    w_uq = jnp.concatenate([uq[..., :NOPE_DIM], uq_rope, _rotate_half_cols(uq_rope)], axis=-1)
    w_uq = w_uq.reshape(Q_LORA, N_HEADS * HEAD_W).astype(BF16)
    dkv_rope = b_w_dkv[0][:, KV_LORA:]
    w_dkv = jnp.concatenate([b_w_dkv[0], _rotate_half_cols(dkv_rope)], axis=-1).astype(BF16)
    ukv = b_w_ukv[0].reshape(KV_LORA, N_HEADS, NOPE_DIM + V_DIM)
    w_uk = ukv[..., :NOPE_DIM].reshape(KV_LORA, N_HEADS * NOPE_DIM).astype(BF16)
    w_uv = ukv[..., NOPE_DIM:].reshape(KV_LORA, N_HEADS * V_DIM).astype(BF16)
    w_uvt = w_uv.T
    w_o = _Layer(b_w_o.astype(BF16), 0)

    outs = []
    for x_in, first_row, shared in ((x_prompt, 0, True), (x_sample, 1, False)):
        st = _Stream(x_in, first_row, shared)
        is_sample = not shared
        x = _gmlp_call(st, st.x, mod4, 0, row(norm1_g[0]), a_w_in_b, row(a_v_g[0]), a_w_s_b, a_b_s_c, a_w_out_b)
        x = _ffn_call(st, x, mod4, 0, row(norm2_g[0]), _Layer(f_w_in_b, 0), _Layer(f_w_out_b, 0))
        proj_head = (row(norm1_g[1]), w_dq, row(b_q_g[0]))
        proj_tail = (w_dkv, row(b_kv_g[0]), w_uk)
        if is_sample:
            qt, k_new, vt_new = _proj_call(st, x, mod4, 1, *proj_head, w_uq.T, *proj_tail, w_uvt,
                                           rope_tables=_axial_rope_tables(st.seq))
            past = cache_ckv.shape[2]
            krz = jnp.pad(cache_krope[:, 0].reshape(st.batch * past, ROPE_DIM), ((0, 0), (0, ROPE_DIM)))
            k_cache, vt_cache = _cachekv_call(
                cache_ckv[:, 0].reshape(st.batch * past, KV_LORA), krz, w_uk, w_uvt)
            attn = _attn_pipe_call(st.batch, st.seq, qt, k_new, vt_new, k_cache, vt_cache)
        else:
            q, k_new, v_new, ckv, kr = _proj_call(st, x, mod4, 1, *proj_head, w_uq, *proj_tail, w_uv)
            attn = _attn_call(st.batch, st.seq, q, k_new, v_new)
            outs += [ckv.reshape(st.batch, 1, st.seq, KV_LORA), kr.reshape(st.batch, 1, st.seq, ROPE_DIM)]
        y = _ffn_call(st, x, mod4, 1, row(norm2_g[1]), _Layer(f_w_in_b, 1), _Layer(f_w_out_b, 1),
                      attn=attn, w_o=w_o, final_g=row(final_g))
        outs.append(y.reshape(x_in.shape))

    new_ckv, new_krope, y_prompt, y_sample = outs
    return (y_prompt, y_sample, new_ckv, new_krope)
```
